```python
import jax, jax.numpy as jnp
from jax import lax
import numpy as np

D_MODEL = 1024
BATCH = 32
SEQ = 2048
DEPTH = 4

CTX_LEN = 256
GRID_W = 64
N_MIXERS = 4
N_LAYERS_A = (DEPTH + 3) // 4
N_LAYERS_B = (DEPTH + 2) // 4
N_LAYERS_C = (DEPTH + 1) // 4
N_LAYERS_D = DEPTH // 4
N_DENSE = (DEPTH + 1) // 2
N_MOE = DEPTH // 2
NORM_EPS = 1e-6
NEG_INF = -1e30

D_RNN = 1280
LRU_BLOCKS = 8
LRU_BW = D_RNN // LRU_BLOCKS
CONV_W = 4
LRU_C = 8.0

D_SGU = 2 * D_MODEL
SGU_CHUNK = 128
SGU_GROUPS = 8
SGU_GD = D_SGU // SGU_GROUPS

POOL_WINDOWS = (2, 4, 8, 16)
POOL_GD = D_MODEL // len(POOL_WINDOWS)

NA_HEADS = 16
NA_HD = D_MODEL // NA_HEADS
NA_KH = 8
NA_KW = 16

D_FF = 7 * D_MODEL // 2
N_EXPERTS = 8
TOP_K = 2

kernel_name = 'hybrid_interleaved_diffusion_block'


def rms_norm(x, g):
    xf = x.astype(jnp.float32)
    y = xf * lax.rsqrt(jnp.mean(xf * xf, axis=-1, keepdims=True) + NORM_EPS)
    return (y * g.astype(jnp.float32)).astype(x.dtype)


def layer_norm(x, g, b):
    xf = x.astype(jnp.float32)
    mu = jnp.mean(xf, axis=-1, keepdims=True)
    var = jnp.mean(jnp.square(xf - mu), axis=-1, keepdims=True)
    y = (xf - mu) * lax.rsqrt(var + NORM_EPS)
    return (y * g.astype(jnp.float32) + b.astype(jnp.float32)).astype(x.dtype)


def modulate(h, shift, scale):
    return h * (1 + scale) + shift


def centred_dwconv(x, w, b):
    n = x.shape[1]
    left = CONV_W // 2
    xp = jnp.pad(x, ((0, 0), (left, CONV_W - 1 - left), (0, 0)))
    y = b + xp[:, 0:n] * w[0]
    for k in range(1, CONV_W):
        y = y + xp[:, k:k + n] * w[k]
    return y


def block_diag_linear(x, w, b):
    bsz, n, _ = x.shape
    xb = x.reshape(bsz, n, LRU_BLOCKS, LRU_BW)
    y = jnp.einsum('bnkc,kcd->bnkd', xb, w.astype(jnp.float32))
    return y.reshape(bsz, n, D_RNN) + b.astype(jnp.float32)


def rglru_coeffs(xc, w_a, b_a, w_i, b_i, lam):
    xf = xc.astype(jnp.float32)
    r = jax.nn.sigmoid(block_diag_linear(xf, w_a, b_a))
    i = jax.nn.sigmoid(block_diag_linear(xf, w_i, b_i))
    log_a = -LRU_C * r * jax.nn.softplus(-lam.astype(jnp.float32))
    a = jnp.exp(log_a)
    mult = jnp.sqrt(-jnp.expm1(2.0 * log_a))
    return a, mult * (i * xf)


def _lin_combine(left, right):
    a1, b1 = left
    a2, b2 = right
    return a1 * a2, a2 * b1 + b2


def linear_scan(a, b, h0=None):
    if h0 is not None:
        b = b.at[:, 0].add(a[:, 0] * h0)
    return lax.associative_scan(_lin_combine, (a, b), axis=1)[1]


def _flip_if(t, rev):
    return jnp.flip(t, axis=1) if rev else t


def mixer_rglru(h_lat, h_ctx, w_in, conv_w, conv_b, w_a, b_a, w_i, b_i, lam, w_out, need_ctx):
    xr_l, gr_l = jnp.split(h_lat @ w_in, 2, axis=-1)
    xc_l = centred_dwconv(xr_l, conv_w, conv_b)
    xc_c = centred_dwconv(h_ctx @ w_in[:, :D_RNN], conv_w, conv_b)
    ys_l, ys_c = [], []
    for d in range(2):
        rev = d == 1
        a_c, b_c = rglru_coeffs(_flip_if(xc_c, rev), w_a[d], b_a[d], w_i[d], b_i[d], lam[d])
        hs_c = linear_scan(a_c, b_c)
        a_l, b_l = rglru_coeffs(_flip_if(xc_l, rev), w_a[d], b_a[d], w_i[d], b_i[d], lam[d])
        hs_l = linear_scan(a_l, b_l, hs_c[:, -1])
        ys_l.append(_flip_if(hs_l, rev))
        ys_c.append(_flip_if(hs_c, rev))
    y_l = ((ys_l[0] + ys_l[1]).astype(h_lat.dtype) * jax.nn.gelu(gr_l)) @ w_out
    y_c = None
    if need_ctx:
        gr_c = h_ctx @ w_in[:, D_RNN:]
        y_c = ((ys_c[0] + ys_c[1]).astype(h_ctx.dtype) * jax.nn.gelu(gr_c)) @ w_out
    return y_l, y_c


def mixer_sgu(h, w_in, b_in, ln_g, ln_b, w_s, b_s, w_out):
    bsz, n, _ = h.shape
    z = jax.nn.gelu(h @ w_in + b_in)
    u, v = jnp.split(z, 2, axis=-1)
    v = layer_norm(v, ln_g, ln_b)
    v = v.reshape(bsz, n // SGU_CHUNK, SGU_CHUNK, SGU_GROUPS, SGU_GD)
    v = jnp.einsum('gpq,bcqgd->bcpgd', w_s, v) + b_s.T[:, :, None]
    return (u * v.reshape(bsz, n, D_SGU)) @ w_out


def centred_mean_minus_self(x, w):
    n = x.shape[1]
    xf = x.astype(jnp.float32)
    cs = jnp.pad(jnp.cumsum(xf, axis=1), ((0, 0), (1, 0), (0, 0)))
    t = jnp.arange(n)
    lo = jnp.clip(t - w // 2, 0, n - 1)
    hi = jnp.clip(t + w // 2 - 1, 0, n - 1)
    s = jnp.take(cs, hi + 1, axis=1) - jnp.take(cs, lo, axis=1)
    cnt = (hi - lo + 1).astype(jnp.float32)[None, :, None]
    return (s / cnt - xf).astype(x.dtype)


def mixer_pool(h, w_g, b_g, ls):
    parts = []
    for g, w in enumerate(POOL_WINDOWS):
        hg = h[..., g * POOL_GD:(g + 1) * POOL_GD]
        parts.append(centred_mean_minus_self(hg, w) @ w_g[g] + b_g[g])
    return jnp.concatenate(parts, axis=-1) * ls


def mixer_na(h_lat, h_ctx, w_qkv, q_g, k_g, rpb, w_o, need_ctx):
    bsz, n, _ = h_lat.shape
    lc = h_ctx.shape[1]
    rows = n // GRID_W
    kh = min(NA_KH, rows)
    scale = NA_HD ** -0.5

    z = (h_lat @ w_qkv).reshape(bsz, rows, GRID_W, 3, NA_HEADS, NA_HD)
    q_l = rms_norm(z[:, :, :, 0], q_g)
    k_l = rms_norm(z[:, :, :, 1], k_g)
    v_l = z[:, :, :, 2]
    if need_ctx:
        zc = (h_ctx @ w_qkv).reshape(bsz, lc, 3, NA_HEADS, NA_HD)
        q_c = rms_norm(zc[:, :, 0], q_g)
        k_c = rms_norm(zc[:, :, 1], k_g)
        v_c = zc[:, :, 2]
    else:
        zc = (h_ctx @ w_qkv[:, D_MODEL:]).reshape(bsz, lc, 2, NA_HEADS, NA_HD)
        k_c = rms_norm(zc[:, :, 0], k_g)
        v_c = zc[:, :, 1]

    col = np.arange(GRID_W)
    col_start = np.clip(col - NA_KW // 2, 0, GRID_W - NA_KW)
    nb_mask = jnp.asarray((col[None, :] >= col_start[:, None]) & (col[None, :] < col_start[:, None] + NA_KW))
    dc_idx = np.clip(col[None, :] - col[:, None], -(NA_KW - 1), NA_KW - 1) + NA_KW - 1
    rpb_cols = rpb.astype(jnp.float32)[:, :, dc_idx]

    def row_block(r):
        r0 = jnp.clip(r - kh // 2, 0, rows - kh)
        q = lax.dynamic_index_in_dim(q_l, r, axis=1, keepdims=False)
        k = lax.dynamic_slice_in_dim(k_l, r0, kh, axis=1)
        v = lax.dynamic_slice_in_dim(v_l, r0, kh, axis=1)
        s_nb = jnp.einsum('bihd,bajhd->bhiaj', q, k, preferred_element_type=jnp.float32) * scale
        bias = lax.dynamic_slice_in_dim(rpb_cols, r0 - r + NA_KH - 1, kh, axis=1)
        s_nb = s_nb + jnp.transpose(bias, (0, 2, 1, 3))[None]
        s_nb = jnp.where(nb_mask[:, None, :], s_nb, NEG_INF)
        s_cx = jnp.einsum('bihd,bmhd->bhim', q, k_c, preferred_element_type=jnp.float32) * scale
        s = jnp.concatenate([s_nb.reshape(bsz, NA_HEADS, GRID_W, kh * GRID_W), s_cx], axis=-1)
        p = jax.nn.softmax(s, axis=-1).astype(v.dtype)
        p_nb = p[..., :kh * GRID_W].reshape(bsz, NA_HEADS, GRID_W, kh, GRID_W)
        o = jnp.einsum('bhiaj,bajhd->bihd', p_nb, v)
        return o + jnp.einsum('bhim,bmhd->bihd', p[..., kh * GRID_W:], v_c)

    o = lax.map(row_block, jnp.arange(rows))
    y_l = jnp.transpose(o, (1, 0, 2, 3, 4)).reshape(bsz, n, D_MODEL) @ w_o
    y_c = None
    if need_ctx:
        s = jnp.einsum('bqhd,bkhd->bhqk', q_c, k_c, preferred_element_type=jnp.float32) * scale
        p = jax.nn.softmax(s, axis=-1).astype(v_c.dtype)
        y_c = jnp.einsum('bhqk,bkhd->bqhd', p, v_c).reshape(bsz, lc, D_MODEL) @ w_o
    return y_l, y_c


def swiglu(h, w1, w3, w2):
    return (jax.nn.silu(h @ w1) * (h @ w3)) @ w2


def moe_swiglu(h, w_r, b_r, w1, w3, w2):
    logits = (h @ w_r).astype(jnp.float32) + b_r.astype(jnp.float32)
    top_v, top_i = lax.top_k(logits, TOP_K)
    top_w = jax.nn.softmax(top_v, axis=-1)
    gates = jnp.sum(jax.nn.one_hot(top_i, N_EXPERTS, dtype=jnp.float32) * top_w[..., None], axis=-2)
    out = None
    for e in range(N_EXPERTS):
        ye = gates[..., e:e + 1].astype(h.dtype) * swiglu(h, w1[e], w3[e], w2[e])
        out = ye if out is None else out + ye
    return out


def setup_inputs(seed: int = 0) -> dict:
    key = jax.random.key(seed)
    ks = iter(jax.random.split(key, 48))

    def nrm(shape, scale):
        return scale * jax.random.normal(next(ks), shape, jnp.float32)

    def gain(shape):
        return 1.0 + 0.02 * jax.random.normal(next(ks), shape, jnp.float32)

    d = D_MODEL
    u = jax.random.uniform(next(ks), (N_LAYERS_A, 2, D_RNN), jnp.float32, 0.9, 0.999)
    s = u ** (1.0 / LRU_C)
    lru_lambda = jnp.log(s) - jnp.log1p(-s)
    return {
        'x': nrm((BATCH, SEQ, d), 1.0),
        'c': nrm((BATCH, d), 1.0),
        'ctx': nrm((BATCH, CTX_LEN, d), 1.0),
        'c_ctx': nrm((d,), 1.0),
        'mod_w': nrm((DEPTH, d, 6 * d), d ** -0.5),
        'mod_b': nrm((DEPTH, 6 * d), 0.01),
        'norm1_g': gain((DEPTH, d)),
        'norm2_g': gain((DEPTH, d)),
        'lru_w_in': nrm((N_LAYERS_A, d, 2 * D_RNN), d ** -0.5),
        'lru_conv_w': nrm((N_LAYERS_A, CONV_W, D_RNN), CONV_W ** -0.5),
        'lru_conv_b': nrm((N_LAYERS_A, D_RNN), 0.01),
        'lru_w_a': nrm((N_LAYERS_A, 2, LRU_BLOCKS, LRU_BW, LRU_BW), LRU_BW ** -0.5),
        'lru_b_a': nrm((N_LAYERS_A, 2, D_RNN), 0.01),
        'lru_w_i': nrm((N_LAYERS_A, 2, LRU_BLOCKS, LRU_BW, LRU_BW), LRU_BW ** -0.5),
        'lru_b_i': nrm((N_LAYERS_A, 2, D_RNN), 0.01),
        'lru_lambda': lru_lambda,
        'lru_w_out': nrm((N_LAYERS_A, D_RNN, d), D_RNN ** -0.5),
        'sgu_w_in': nrm((N_LAYERS_B, d, 2 * D_SGU), d ** -0.5),
        'sgu_b_in': nrm((N_LAYERS_B, 2 * D_SGU), 0.01),
        'sgu_ln_g': gain((N_LAYERS_B, D_SGU)),
        'sgu_ln_b': nrm((N_LAYERS_B, D_SGU), 0.01),
        'sgu_w_s': nrm((N_LAYERS_B, SGU_GROUPS, SGU_CHUNK, SGU_CHUNK), SGU_CHUNK ** -0.5),
        'sgu_b_s': gain((N_LAYERS_B, SGU_GROUPS, SGU_CHUNK)),
        'sgu_w_out': nrm((N_LAYERS_B, D_SGU, d), D_SGU ** -0.5),
        'pool_w': nrm((N_LAYERS_C, len(POOL_WINDOWS), POOL_GD, POOL_GD), POOL_GD ** -0.5),
        'pool_b': nrm((N_LAYERS_C, len(POOL_WINDOWS), POOL_GD), 0.01),
        'pool_scale': gain((N_LAYERS_C, d)),
        'na_w_qkv': nrm((N_LAYERS_D, d, 3 * d), d ** -0.5),
        'na_q_g': gain((N_LAYERS_D, NA_HD)),
        'na_k_g': gain((N_LAYERS_D, NA_HD)),
        'na_rpb': nrm((N_LAYERS_D, NA_HEADS, 2 * NA_KH - 1, 2 * NA_KW - 1), 0.02),
        'na_w_o': nrm((N_LAYERS_D, d, d), d ** -0.5),
        'ffn_w1': nrm((N_DENSE, d, D_FF), d ** -0.5),
        'ffn_w3': nrm((N_DENSE, d, D_FF), d ** -0.5),
        'ffn_w2': nrm((N_DENSE, D_FF, d), D_FF ** -0.5),
        'moe_w_r': nrm((N_MOE, d, N_EXPERTS), d ** -0.5),
        'moe_b_r': nrm((N_MOE, N_EXPERTS), 0.01),
        'moe_w1': nrm((N_MOE, N_EXPERTS, d, D_FF), d ** -0.5),
        'moe_w3': nrm((N_MOE, N_EXPERTS, d, D_FF), d ** -0.5),
        'moe_w2': nrm((N_MOE, N_EXPERTS, D_FF, d), D_FF ** -0.5),
    }


def channel_mixer(f, i, ffn_w1, ffn_w3, ffn_w2, moe_w_r, moe_b_r, moe_w1, moe_w3, moe_w2):
    k = i // 2
    if i % 2 == 0:
        return swiglu(f, ffn_w1[k], ffn_w3[k], ffn_w2[k])
    return moe_swiglu(f, moe_w_r[k], moe_b_r[k], moe_w1[k], moe_w3[k], moe_w2[k])


def reference(x, c, ctx, c_ctx, mod_w, mod_b, norm1_g, norm2_g,
              lru_w_in, lru_conv_w, lru_conv_b, lru_w_a, lru_b_a, lru_w_i, lru_b_i, lru_lambda, lru_w_out,
              sgu_w_in, sgu_b_in, sgu_ln_g, sgu_ln_b, sgu_w_s, sgu_b_s, sgu_w_out,
              pool_w, pool_b, pool_scale,
              na_w_qkv, na_q_g, na_k_g, na_rpb, na_w_o,
              ffn_w1, ffn_w3, ffn_w2,
              moe_w_r, moe_b_r, moe_w1, moe_w3, moe_w2):
    h_lat, h_ctx = x, ctx
    s_lat = jax.nn.silu(c)[:, None, :]
    s_ctx = jax.nn.silu(c_ctx)[None, None, :]
    for i in range(DEPTH):
        kind, j = i % N_MIXERS, i // N_MIXERS
        ctx_out = i != DEPTH - 1
        ctx_in = ctx_out or kind in (0, 3)
        m_l = jnp.split(s_lat @ mod_w[i] + mod_b[i], 6, axis=-1)
        u_l = modulate(rms_norm(h_lat, norm1_g[i]), m_l[0], m_l[1])
        m_c, u_c = None, None
        if ctx_in:
            m_c = jnp.split(s_ctx @ mod_w[i] + mod_b[i], 6, axis=-1)
            u_c = modulate(rms_norm(h_ctx, norm1_g[i]), m_c[0], m_c[1])
        if kind == 0:
            y_l, y_c = mixer_rglru(u_l, u_c, lru_w_in[j], lru_conv_w[j], lru_conv_b[j], lru_w_a[j], lru_b_a[j],
                                   lru_w_i[j], lru_b_i[j], lru_lambda[j], lru_w_out[j], ctx_out)
        elif kind == 1:
            y_l = mixer_sgu(u_l, sgu_w_in[j], sgu_b_in[j], sgu_ln_g[j], sgu_ln_b[j], sgu_w_s[j], sgu_b_s[j], sgu_w_out[j])
            y_c = mixer_sgu(u_c, sgu_w_in[j], sgu_b_in[j], sgu_ln_g[j], sgu_ln_b[j], sgu_w_s[j], sgu_b_s[j], sgu_w_out[j]) if ctx_out else None
        elif kind == 2:
            y_l = mixer_pool(u_l, pool_w[j], pool_b[j], pool_scale[j])
            y_c = mixer_pool(u_c, pool_w[j], pool_b[j], pool_scale[j]) if ctx_out else None
        else:
            y_l, y_c = mixer_na(u_l, u_c, na_w_qkv[j], na_q_g[j], na_k_g[j], na_rpb[j], na_w_o[j], ctx_out)
        h_lat = h_lat + m_l[2] * y_l
        f_l = modulate(rms_norm(h_lat, norm2_g[i]), m_l[3], m_l[4])
        h_lat = h_lat + m_l[5] * channel_mixer(f_l, i, ffn_w1, ffn_w3, ffn_w2, moe_w_r, moe_b_r, moe_w1, moe_w3, moe_w2)
        if ctx_out:
            h_ctx = h_ctx + m_c[2] * y_c
            f_c = modulate(rms_norm(h_ctx, norm2_g[i]), m_c[3], m_c[4])
            h_ctx = h_ctx + m_c[5] * channel_mixer(f_c, i, ffn_w1, ffn_w3, ffn_w2, moe_w_r, moe_b_r, moe_w1, moe_w3, moe_w2)
    return h_lat
```

```python
import functools

import jax
import jax.numpy as jnp
import numpy as np
from jax import lax
from jax.experimental import pallas as pl
from jax.experimental.pallas import tpu as pltpu

F32 = jnp.float32
BF16 = jnp.bfloat16

D_MODEL = 1024
SEQ = 2048
CTX_LEN = 256
GRID_W = 64
NORM_EPS = 1e-6
NEG_INF = -1e30

D_RNN = 1280
LRU_BLOCKS = 8
LRU_BW = D_RNN // LRU_BLOCKS
LRU_HALF = D_RNN // 2
CONV_W = 4
LRU_C = 8.0
SCAN_CHUNK = 256

D_SGU = 2 * D_MODEL
SGU_CHUNK = 128
SGU_GROUPS = 8
SGU_GD = D_SGU // SGU_GROUPS

POOL_WINDOWS = (2, 4, 8, 16)
POOL_GD = D_MODEL // len(POOL_WINDOWS)

NA_HEADS = 16
NA_HD = D_MODEL // NA_HEADS
NA_KH = 8
NA_KW = 16
NA_ROWS = SEQ // GRID_W

D_FF = 7 * D_MODEL // 2
N_EXPERTS = 8
TOP_K = 2

LANES = 128
SUBLANES = 8
MXU_DIM = 256
VMEM_LIMIT = 56 * 2**20

TM_TOK = 512
TM_FFN = 1024
TF_FFN = 512


def _cparams(sem):
    return pltpu.CompilerParams(dimension_semantics=sem, vmem_limit_bytes=VMEM_LIMIT)


def _dot(a, b):
    return jnp.dot(a, b, preferred_element_type=F32)


def _dot_nt(a, b):
    return lax.dot_general(a, b, (((1,), (1,)), ((), ())), preferred_element_type=F32)


def _split_bf16(x):
    hi = x.astype(BF16)
    lo = (x - hi.astype(F32)).astype(BF16)
    return hi, lo


def _normmod(h, g, shift, scale):
    ms = jnp.mean(h * h, axis=-1, keepdims=True)
    y = (h * lax.rsqrt(ms + NORM_EPS)) * g
    return y * (1.0 + scale) + shift


def _mod_index(layer, n_mod_rows, n_batch, tm):
    def index_map(i, *_):
        return (layer * n_mod_rows + jnp.minimum((i * tm) // SEQ, n_batch), 0, 0)
    return index_map


def _mod_kernel(s_ref, w_ref, b_ref, o_ref):
    s = s_ref[...]
    s = s * jax.nn.sigmoid(s)
    s_hi, s_lo = _split_bf16(s)
    w_hi, w_lo = _split_bf16(w_ref[0])
    acc = _dot(s_hi, w_hi) + _dot(s_lo, w_hi) + _dot(s_hi, w_lo)
    o_ref[0] = acc + b_ref[0]


def _mod_table(c_all, mod_w, mod_b):
    depth, d, n6 = mod_w.shape
    rows = c_all.shape[0]
    tn = 1024
    out = pl.pallas_call(
        _mod_kernel,
        grid=(depth, n6 // tn),
        in_specs=[
            pl.BlockSpec((rows, d), lambda l, j: (0, 0)),
            pl.BlockSpec((1, d, tn), lambda l, j: (l, 0, j)),
            pl.BlockSpec((1, 1, tn), lambda l, j: (l, 0, j)),
        ],
        out_specs=pl.BlockSpec((1, rows, tn), lambda l, j: (l, 0, j)),
        out_shape=jax.ShapeDtypeStruct((depth, rows, n6), F32),
        compiler_params=_cparams(("arbitrary", "arbitrary")),
        name="mod_table",
    )(c_all, mod_w, mod_b.reshape(depth, 1, n6))
    out = out.reshape(depth, rows, 6, d)
    out = jnp.pad(out, ((0, 0), (0, 0), (0, 2), (0, 0)))
    return out.reshape(depth * rows, 8, d)


def _mm_res_kernel(gate_row, y_ref, w_ref, h_ref, m_ref, o_ref):
    m = m_ref[0]
    o_ref[...] = h_ref[...] + m[gate_row:gate_row + 1] * _dot(y_ref[...], w_ref[...])


def _mm_res(y, w, h, mod, mod_map, n_rows, gate_row, name):
    tm = TM_TOK
    k, n = w.shape
    return pl.pallas_call(
        functools.partial(_mm_res_kernel, gate_row),
        grid=(n_rows // tm,),
        in_specs=[
            pl.BlockSpec((tm, k), lambda i: (i, 0)),
            pl.BlockSpec((k, n), lambda i: (0, 0)),
            pl.BlockSpec((tm, n), lambda i: (i, 0)),
            pl.BlockSpec((1, 8, n), mod_map),
        ],
        out_specs=pl.BlockSpec((tm, n), lambda i: (i, 0)),
        out_shape=jax.ShapeDtypeStruct((n_rows, n), F32),
        compiler_params=_cparams(("arbitrary",)),
        name=name,
    )(y, w, h, mod)


def _ffn_kernel(nf, h_ref, m_ref, g_ref, w1_ref, w3_ref, w2_ref, o_ref, f_s, acc_s):
    j = pl.program_id(1)

    @pl.when(j == 0)
    def _():
        m = m_ref[0]
        f_s[...] = _normmod(h_ref[...], g_ref[...], m[3:4], m[4:5]).astype(BF16)
        acc_s[...] = jnp.zeros_like(acc_s)

    f = f_s[...]
    h1 = _dot(f, w1_ref[...])
    h3 = _dot(f, w3_ref[...])
    hid = (h1 * jax.nn.sigmoid(h1) * h3).astype(BF16)
    acc_s[...] += _dot(hid, w2_ref[...])

    @pl.when(j == nf - 1)
    def _():
        o_ref[...] = h_ref[...] + m_ref[0][5:6] * acc_s[...]


def _ffn(h, mod, mod_map, g2, w1, w3, w2, n_rows, tm):
    d, dff = w1.shape
    tf = TF_FFN
    nf = dff // tf
    return pl.pallas_call(
        functools.partial(_ffn_kernel, nf),
        grid=(n_rows // tm, nf),
        in_specs=[
            pl.BlockSpec((tm, d), lambda i, j: (i, 0)),
            pl.BlockSpec((1, 8, d), mod_map),
            pl.BlockSpec((1, d), lambda i, j: (0, 0)),
            pl.BlockSpec((d, tf), lambda i, j: (0, j)),
            pl.BlockSpec((d, tf), lambda i, j: (0, j)),
            pl.BlockSpec((tf, d), lambda i, j: (j, 0)),
        ],
        out_specs=pl.BlockSpec((tm, d), lambda i, j: (i, 0)),
        out_shape=jax.ShapeDtypeStruct((n_rows, d), F32),
        scratch_shapes=[pltpu.VMEM((tm, d), BF16), pltpu.VMEM((tm, d), F32)],
        compiler_params=_cparams(("arbitrary", "arbitrary")),
        name="ffn_dense",
    )(h, mod, g2, w1, w3, w2)


def _router_kernel(h_ref, m_ref, g_ref, wh_ref, wl_ref, b_ref, f_ref, ti_ref, tw_ref):
    m = m_ref[0]
    f = _normmod(h_ref[...], g_ref[...], m[3:4], m[4:5])
    f_ref[...] = f
    f_hi, f_lo = _split_bf16(f)
    logits = _dot(f_hi, wh_ref[...]) + _dot(f_lo, wh_ref[...]) + _dot(f_hi, wl_ref[...])
    logits = logits + b_ref[...]
    lane = lax.broadcasted_iota(jnp.int32, logits.shape, 1)
    v1 = jnp.max(logits, axis=-1, keepdims=True)
    i1 = jnp.min(jnp.where(logits == v1, lane, LANES), axis=-1, keepdims=True)
    rest = jnp.where(lane == i1, -jnp.inf, logits)
    v2 = jnp.max(rest, axis=-1, keepdims=True)
    i2 = jnp.min(jnp.where(rest == v2, lane, LANES), axis=-1, keepdims=True)
    e2 = jnp.exp(v2 - v1)
    w1 = 1.0 / (1.0 + e2)
    w2 = e2 / (1.0 + e2)
    col = lax.broadcasted_iota(jnp.int32, ti_ref.shape, 1)
    ti_ref[...] = jnp.where(col == 0, i1, i2)
    tw_ref[...] = jnp.where(col == 0, w1, w2)


def _router(h, mod, mod_map, g2, wr_hi, wr_lo, br, n_rows):
    tm = TM_TOK
    d = h.shape[1]
    return pl.pallas_call(
        _router_kernel,
        grid=(n_rows // tm,),
        in_specs=[
            pl.BlockSpec((tm, d), lambda i: (i, 0)),
            pl.BlockSpec((1, 8, d), mod_map),
            pl.BlockSpec((1, d), lambda i: (0, 0)),
            pl.BlockSpec((d, LANES), lambda i: (0, 0)),
            pl.BlockSpec((d, LANES), lambda i: (0, 0)),
            pl.BlockSpec((1, LANES), lambda i: (0, 0)),
        ],
        out_specs=[
            pl.BlockSpec((tm, d), lambda i: (i, 0)),
            pl.BlockSpec((tm, TOP_K), lambda i: (i, 0)),
            pl.BlockSpec((tm, TOP_K), lambda i: (i, 0)),
        ],
        out_shape=[
            jax.ShapeDtypeStruct((n_rows, d), F32),
            jax.ShapeDtypeStruct((n_rows, TOP_K), jnp.int32),
            jax.ShapeDtypeStruct((n_rows, TOP_K), F32),
        ],
        compiler_params=_cparams(("arbitrary",)),
        name="moe_router",
    )(h, mod, g2, wr_hi, wr_lo, br)


def _row_copy(src_hbm, dst_hbm_or_vmem, src_row, dst_row, sem):
    return pltpu.make_async_copy(src_hbm.at[pl.ds(src_row, 1)], dst_hbm_or_vmem.at[pl.ds(dst_row, 1)], sem)


def _moe_kernel(nf, tm, te_ref, act_ref, src_ref, dst_ref, gw_ref, f_hbm, w1_ref, w3_ref, w2_ref,
                o_hbm, xg_s, x_s, acc_s, y_s, sem):
    i = pl.program_id(0)
    j = pl.program_id(1)
    active = act_ref[i] > 0

    @pl.when(jnp.logical_and(active, j == 0))
    def _():
        def start(r, c):
            _row_copy(f_hbm, xg_s, src_ref[0, 0, r], r, sem.at[0]).start()
            return c
        lax.fori_loop(0, tm, start, 0)

        def wait(r, c):
            _row_copy(f_hbm, xg_s, src_ref[0, 0, r], r, sem.at[0]).wait()
            return c
        lax.fori_loop(0, tm, wait, 0)
        x_s[...] = xg_s[...].astype(BF16)
        acc_s[...] = jnp.zeros_like(acc_s)

    @pl.when(active)
    def _():
        x = x_s[...]
        h1 = _dot(x, w1_ref[0])
        h3 = _dot(x, w3_ref[0])
        hid = (h1 * jax.nn.sigmoid(h1) * h3).astype(BF16)
        acc_s[...] += _dot(hid, w2_ref[0])

    @pl.when(jnp.logical_and(active, j == nf - 1))
    def _():
        y_s[...] = gw_ref[...] * acc_s[...]

        def start(r, c):
            _row_copy(y_s, o_hbm, r, dst_ref[0, 0, r], sem.at[1]).start()
            return c
        lax.fori_loop(0, tm, start, 0)

        def wait(r, c):
            _row_copy(y_s, o_hbm, r, dst_ref[0, 0, r], sem.at[1]).wait()
            return c
        lax.fori_loop(0, tm, wait, 0)


def _moe_plan(top_i, top_w, tm):
    n_tok = top_i.shape[0]
    n_asg = n_tok * TOP_K
    n_tiles = n_asg // tm + N_EXPERTS
    n_pos = n_tiles * tm
    flat_e = top_i.reshape(-1)
    order = jnp.argsort(flat_e, stable=True).astype(jnp.int32)
    counts = jnp.sum(flat_e[:, None] == jnp.arange(N_EXPERTS, dtype=jnp.int32)[None, :], axis=0).astype(jnp.int32)
    padded = ((counts + tm - 1) // tm) * tm
    pad_end = jnp.cumsum(padded)
    pad_off = pad_end - padded
    start = jnp.cumsum(counts) - counts
    n_used = pad_end[-1] // tm
    tile = jnp.arange(n_tiles, dtype=jnp.int32)
    tile_e = jnp.minimum(jnp.searchsorted(pad_end, tile * tm, side="right"), N_EXPERTS - 1).astype(jnp.int32)
    active = (tile < n_used).astype(jnp.int32)
    last_e = tile_e[jnp.maximum(n_used - 1, 0)]
    tile_e = jnp.where(active > 0, tile_e, last_e)
    pos = jnp.arange(n_pos, dtype=jnp.int32)
    e_p = tile_e[pos // tm]
    jj = pos - pad_off[e_p]
    valid = jnp.logical_and(jj < counts[e_p], active[pos // tm] > 0)
    asg = order[jnp.clip(start[e_p] + jj, 0, n_asg - 1)]
    pad_rank = jnp.cumsum(jnp.logical_not(valid).astype(jnp.int32)) - 1
    src_tok = jnp.where(valid, asg // TOP_K, 0).astype(jnp.int32)
    dst_row = jnp.where(valid, asg, n_asg + pad_rank).astype(jnp.int32)
    gate = jnp.where(valid, top_w.reshape(-1)[asg], 0.0).astype(F32)
    return (tile_e, active, src_tok.reshape(n_tiles, 1, tm), dst_row.reshape(n_tiles, 1, tm),
            gate.reshape(n_pos, 1), n_tiles, n_pos)


def _moe_experts(f, plan, w1, w3, w2, tm):
    tile_e, active, src_tok, dst_row, gate, n_tiles, n_pos = plan
    _, d, dff = w1.shape
    tf = TF_FFN
    nf = dff // tf

    def w_col(i, j, te, act):
        return (te[i], 0, jnp.where(act[i] > 0, j, nf - 1))

    def w_row(i, j, te, act):
        return (te[i], jnp.where(act[i] > 0, j, nf - 1), 0)

    grid_spec = pltpu.PrefetchScalarGridSpec(
        num_scalar_prefetch=2,
        grid=(n_tiles, nf),
        in_specs=[
            pl.BlockSpec((1, 1, tm), lambda i, j, te, act: (i, 0, 0), memory_space=pltpu.SMEM),
            pl.BlockSpec((1, 1, tm), lambda i, j, te, act: (i, 0, 0), memory_space=pltpu.SMEM),
            pl.BlockSpec((tm, 1), lambda i, j, te, act: (i, 0)),
            pl.BlockSpec(memory_space=pl.ANY),
            pl.BlockSpec((1, d, tf), w_col),
            pl.BlockSpec((1, d, tf), w_col),
            pl.BlockSpec((1, tf, d), w_row),
        ],
        out_specs=pl.BlockSpec(memory_space=pl.ANY),
        scratch_shapes=[
            pltpu.VMEM((tm, d), F32),
            pltpu.VMEM((tm, d), BF16),
            pltpu.VMEM((tm, d), F32),
            pltpu.VMEM((tm, d), F32),
            pltpu.SemaphoreType.DMA((2,)),
        ],
    )
    return pl.pallas_call(
        functools.partial(_moe_kernel, nf, tm),
        grid_spec=grid_spec,
        out_shape=jax.ShapeDtypeStruct((n_pos, d), F32),
        compiler_params=_cparams(("arbitrary", "arbitrary")),
        name="moe_experts",
    )(tile_e, active, src_tok, dst_row, gate, f, w1, w3, w2)


def _moe_combine_kernel(h_ref, m_ref, y_ref, o_ref):
    d = h_ref.shape[1]
    y = y_ref[:, :d] + y_ref[:, d:]
    o_ref[...] = h_ref[...] + m_ref[0][5:6] * y


def _moe_combine(h, mod, mod_map, y_pairs, n_rows):
    tm = TM_TOK
    d = h.shape[1]
    return pl.pallas_call(
        _moe_combine_kernel,
        grid=(n_rows // tm,),
        in_specs=[
            pl.BlockSpec((tm, d), lambda i: (i, 0)),
            pl.BlockSpec((1, 8, d), mod_map),
            pl.BlockSpec((tm, TOP_K * d), lambda i: (i, 0)),
        ],
        out_specs=pl.BlockSpec((tm, d), lambda i: (i, 0)),
        out_shape=jax.ShapeDtypeStruct((n_rows, d), F32),
        compiler_params=_cparams(("arbitrary",)),
        name="moe_combine",
    )(h, mod, y_pairs)


def _moe(h, mod, mod_map, g2, w_r, b_r, w1, w3, w2, n_rows, tm):
    d = h.shape[1]
    wr_pad = jnp.pad(w_r, ((0, 0), (0, LANES - N_EXPERTS)))
    wr_hi, wr_lo = _split_bf16(wr_pad)
    br_pad = jnp.concatenate([b_r.astype(F32), jnp.full((LANES - N_EXPERTS,), NEG_INF, F32)]).reshape(1, LANES)
    f, top_i, top_w = _router(h, mod, mod_map, g2, wr_hi, wr_lo, br_pad, n_rows)
    plan = _moe_plan(top_i, top_w, tm)
    y_rows = _moe_experts(f, plan, w1, w3, w2, tm)
    y_pairs = y_rows[:n_rows * TOP_K].reshape(n_rows, TOP_K * d)
    return _moe_combine(h, mod, mod_map, y_pairs, n_rows)


def _lru_in_kernel(h_ref, m_ref, g_ref, w_ref, xr_ref, gg_ref):
    m = m_ref[0]
    u = _normmod(h_ref[...], g_ref[...], m[0:1], m[1:2]).astype(BF16)
    xr_ref[...] = _dot(u, w_ref[:, :D_RNN])
    gg_ref[...] = jax.nn.gelu(_dot(u, w_ref[:, D_RNN:])).astype(BF16)


def _lru_in(h, mod, mod_map, g1, w_in, n_rows):
    tm = TM_TOK
    d = h.shape[1]
    return pl.pallas_call(
        _lru_in_kernel,
        grid=(n_rows // tm,),
        in_specs=[
            pl.BlockSpec((tm, d), lambda i: (i, 0)),
            pl.BlockSpec((1, 8, d), mod_map),
            pl.BlockSpec((1, d), lambda i: (0, 0)),
            pl.BlockSpec((d, 2 * D_RNN), lambda i: (0, 0)),
        ],
        out_specs=[
            pl.BlockSpec((tm, D_RNN), lambda i: (i, 0)),
            pl.BlockSpec((tm, D_RNN), lambda i: (i, 0)),
        ],
        out_shape=[
            jax.ShapeDtypeStruct((n_rows, D_RNN), F32),
            jax.ShapeDtypeStruct((n_rows, D_RNN), BF16),
        ],
        compiler_params=_cparams(("arbitrary",)),
        name="lru_in",
    )(h, mod, g1, w_in)


def _block_scan(a, b, h0, reverse):
    n, c = a.shape
    groups = n // SUBLANES
    a3 = a.reshape(groups, SUBLANES, c)
    b3 = b.reshape(groups, SUBLANES, c)
    row = lax.broadcasted_iota(jnp.int32, a3.shape, 1)
    for s in (1, 2, 4):
        if reverse:
            keep = row < SUBLANES - s
            shift = SUBLANES - s
        else:
            keep = row >= s
            shift = s
        a_prev = jnp.where(keep, pltpu.roll(a3, shift, axis=1), 1.0)
        b_prev = jnp.where(keep, pltpu.roll(b3, shift, axis=1), 0.0)
        b3 = a3 * b_prev + b3
        a3 = a3 * a_prev
    outs = [None] * groups
    h = h0
    for g in (range(groups - 1, -1, -1) if reverse else range(groups)):
        hs = b3[g] + a3[g] * h
        outs[g] = hs
        h = hs[0:1] if reverse else hs[SUBLANES - 1:SUBLANES]
    return jnp.concatenate(outs, axis=0), h


def _lru_scan_kernel(n, xr_ref, gg_ref, cw_ref, cb_ref, wp_ref, bias_ref, lam_ref, seed_ref,
                     y_ref, fin_ref, xp_s, ab_s, bb_s, hf_s):
    pad = SUBLANES
    c = xr_ref.shape[1]
    n_chunks = n // SCAN_CHUNK
    win_rows = SCAN_CHUNK + 2 * pad

    xp_s[0:pad, :] = jnp.zeros((pad, c), F32)
    xp_s[pad:pad + n, :] = xr_ref[...]
    xp_s[pad + n:2 * pad + n, :] = jnp.zeros((pad, c), F32)

    lam = lam_ref[...]
    nlam = -lam
    softplus = jnp.maximum(nlam, 0.0) + jnp.log1p(jnp.exp(-jnp.abs(nlam)))
    decay = -LRU_C * softplus
    cw = cw_ref[...]
    cb = cb_ref[...]
    bias = bias_ref[...]

    def coeffs(ci):
        win = xp_s[pl.ds(pl.multiple_of(ci * SCAN_CHUNK, SCAN_CHUNK), win_rows), :]
        xc = cb
        for k in range(CONV_W):
            shift = (CONV_W // 2 - k) % win_rows
            tap = win if shift == 0 else pltpu.roll(win, shift, axis=0)
            xc = xc + tap[pad:pad + SCAN_CHUNK] * cw[k:k + 1]
        xcb = xc.astype(BF16)
        prods = [_dot(xcb[:, LANES * w:LANES * w + MXU_DIM], wp_ref[0, w]) for w in range(4)]

        def assemble(q):
            p = [r[:, MXU_DIM * q:MXU_DIM * (q + 1)] for r in prods]
            return jnp.concatenate(
                [p[0][:, :LANES], p[0][:, LANES:] + p[1][:, :LANES], p[1][:, LANES:] + p[2][:, :LANES],
                 p[2][:, LANES:] + p[3][:, :LANES], p[3][:, LANES:]], axis=1)

        out = []
        for d in range(2):
            r = jax.nn.sigmoid(assemble(2 * d) + bias[2 * d:2 * d + 1])
            gi = jax.nn.sigmoid(assemble(2 * d + 1) + bias[2 * d + 1:2 * d + 2])
            log_a = decay[d:d + 1] * r
            a = jnp.exp(log_a)
            mult = jnp.sqrt(1.0 - jnp.exp(2.0 * log_a))
            out.append((a, mult * (gi * xc)))
        return out

    def fwd_body(ci, h):
        (a0, b0), (a1, b1) = coeffs(ci)
        rows = pl.ds(pl.multiple_of(ci * SCAN_CHUNK, SCAN_CHUNK), SCAN_CHUNK)
        ab_s[rows, :] = a1
        bb_s[rows, :] = b1
        hs, h = _block_scan(a0, b0, h, reverse=False)
        hf_s[rows, :] = hs
        return h

    h_fwd = lax.fori_loop(0, n_chunks, fwd_body, seed_ref[0, 0:1, :])

    def bwd_body(k, h):
        ci = n_chunks - 1 - k
        rows = pl.ds(pl.multiple_of(ci * SCAN_CHUNK, SCAN_CHUNK), SCAN_CHUNK)
        hs, h = _block_scan(ab_s[rows, :], bb_s[rows, :], h, reverse=True)
        y_ref[rows, :] = ((hf_s[rows, :] + hs) * gg_ref[rows, :].astype(F32)).astype(BF16)
        return h

    h_bwd = lax.fori_loop(0, n_chunks, bwd_body, seed_ref[0, 1:2, :])
    fin_ref[0] = jnp.concatenate([h_fwd, h_bwd, jnp.zeros((SUBLANES - 2, c), F32)], axis=0)


def _lru_scan(xr, gg, conv_w, conv_b, wp, bias, lam, seeds, n_batch, n, blk0, name):
    c = LRU_HALF
    seq_in = lambda b, hf: (blk0 + b, hf)
    return pl.pallas_call(
        functools.partial(_lru_scan_kernel, n),
        grid=(n_batch, 2),
        in_specs=[
            pl.BlockSpec((n, c), seq_in),
            pl.BlockSpec((n, c), seq_in),
            pl.BlockSpec((CONV_W, c), lambda b, hf: (0, hf)),
            pl.BlockSpec((1, c), lambda b, hf: (0, hf)),
            pl.BlockSpec((1, 4, MXU_DIM, 4 * MXU_DIM), lambda b, hf: (hf, 0, 0, 0)),
            pl.BlockSpec((4, c), lambda b, hf: (0, hf)),
            pl.BlockSpec((2, c), lambda b, hf: (0, hf)),
            pl.BlockSpec((1, SUBLANES, c), lambda b, hf: (b, 0, hf)),
        ],
        out_specs=[
            pl.BlockSpec((n, c), lambda b, hf: (b, hf)),
            pl.BlockSpec((1, SUBLANES, c), lambda b, hf: (b, 0, hf)),
        ],
        out_shape=[
            jax.ShapeDtypeStruct((n_batch * n, D_RNN), BF16),
            jax.ShapeDtypeStruct((n_batch, SUBLANES, D_RNN), F32),
        ],
        scratch_shapes=[
            pltpu.VMEM((n + 2 * SUBLANES, c), F32),
            pltpu.VMEM((n, c), F32),
            pltpu.VMEM((n, c), F32),
            pltpu.VMEM((n, c), F32),
        ],
        compiler_params=_cparams(("arbitrary", "arbitrary")),
        name=name,
    )(xr, gg, conv_w, conv_b, wp, bias, lam, seeds)


def _lru_gate_weights(w_a, b_a, w_i, b_i):
    kinds = [w_a[0], w_i[0], w_a[1], w_i[1]]
    wp = jnp.zeros((2, 4, MXU_DIM, 4 * MXU_DIM), F32)
    for hf in range(2):
        for w in range(4):
            off = (LRU_BW - LANES) * w
            for q, kw in enumerate(kinds):
                wp = wp.at[hf, w, off:off + LRU_BW, MXU_DIM * q + off:MXU_DIM * q + off + LRU_BW].set(kw[4 * hf + w])
    bias = jnp.stack([b_a[0], b_i[0], b_a[1], b_i[1]], axis=0).astype(F32)
    return wp.astype(BF16), bias


def _mixer_rglru(h, mod, mod_map, g1, p, n_batch, n_rows):
    xr, gg = _lru_in(h, mod, mod_map, g1, p["w_in"], n_rows)
    zeros = jnp.zeros((n_batch, SUBLANES, D_RNN), F32)
    scan_args = (p["conv_w"], p["conv_b"], p["wp"], p["bias"], p["lam"])
    y_ctx, fin = _lru_scan(xr, gg, *scan_args, zeros, n_batch, CTX_LEN, n_batch * SEQ // CTX_LEN, "lru_scan_ctx")
    y_lat, _ = _lru_scan(xr, gg, *scan_args, fin, n_batch, SEQ, 0, "lru_scan_lat")
    y = jnp.concatenate([y_lat, y_ctx], axis=0)
    return _mm_res(y, p["w_out"], h, mod, mod_map, n_rows, 2, "lru_out")


def _sgu_in_kernel(h_ref, m_ref, g_ref, w_ref, b_ref, lng_ref, lnb_ref, u_ref, v_ref, x_s):
    j = pl.program_id(1)

    @pl.when(j == 0)
    def _():
        m = m_ref[0]
        x_s[...] = _normmod(h_ref[...], g_ref[...], m[0:1], m[1:2]).astype(BF16)

    z = jax.nn.gelu(_dot(x_s[...], w_ref[...]) + b_ref[...])

    @pl.when(j == 0)
    def _():
        u_ref[...] = z.astype(BF16)

    @pl.when(j == 1)
    def _():
        mu = jnp.mean(z, axis=-1, keepdims=True)
        zc = z - mu
        var = jnp.mean(zc * zc, axis=-1, keepdims=True)
        v_ref[...] = (zc * lax.rsqrt(var + NORM_EPS) * lng_ref[...] + lnb_ref[...]).astype(BF16)


def _sgu_in(h, mod, mod_map, g1, w_in, b_in, ln_g, ln_b, n_rows):
    tm = TM_TOK
    d = h.shape[1]
    return pl.pallas_call(
        _sgu_in_kernel,
        grid=(n_rows // tm, 2),
        in_specs=[
            pl.BlockSpec((tm, d), lambda i, j: (i, 0)),
            pl.BlockSpec((1, 8, d), mod_map),
            pl.BlockSpec((1, d), lambda i, j: (0, 0)),
            pl.BlockSpec((d, D_SGU), lambda i, j: (0, j)),
            pl.BlockSpec((1, D_SGU), lambda i, j: (0, j)),
            pl.BlockSpec((1, D_SGU), lambda i, j: (0, 0)),
            pl.BlockSpec((1, D_SGU), lambda i, j: (0, 0)),
        ],
        out_specs=[
            pl.BlockSpec((tm, D_SGU), lambda i, j: (i, 0)),
            pl.BlockSpec((tm, D_SGU), lambda i, j: (i, 0)),
        ],
        out_shape=[
            jax.ShapeDtypeStruct((n_rows, D_SGU), BF16),
            jax.ShapeDtypeStruct((n_rows, D_SGU), BF16),
        ],
        scratch_shapes=[pltpu.VMEM((tm, d), BF16)],
        compiler_params=_cparams(("arbitrary", "arbitrary")),
        name="sgu_in",
    )(h, mod, g1, w_in, b_in, ln_g, ln_b)


def _sgu_mix_kernel(u_ref, v_ref, ws_ref, bs_ref, wo_ref, h_ref, m_ref, o_ref, t_s):
    tm = u_ref.shape[0]
    for ck in range(tm // SGU_CHUNK):
        rows = slice(ck * SGU_CHUNK, (ck + 1) * SGU_CHUNK)
        for g in range(SGU_GROUPS):
            cols = slice(g * SGU_GD, (g + 1) * SGU_GD)
            mixed = _dot(ws_ref[g], v_ref[rows, cols]) + bs_ref[g]
            t_s[rows, cols] = (u_ref[rows, cols].astype(F32) * mixed).astype(BF16)
    o_ref[...] = h_ref[...] + m_ref[0][2:3] * _dot(t_s[...], wo_ref[...])


def _sgu_mix(u, v, w_s, b_s, w_out, h, mod, mod_map, n_rows):
    tm = TM_TOK
    d = h.shape[1]
    return pl.pallas_call(
        _sgu_mix_kernel,
        grid=(n_rows // tm,),
        in_specs=[
            pl.BlockSpec((tm, D_SGU), lambda i: (i, 0)),
            pl.BlockSpec((tm, D_SGU), lambda i: (i, 0)),
            pl.BlockSpec((SGU_GROUPS, SGU_CHUNK, SGU_CHUNK), lambda i: (0, 0, 0)),
            pl.BlockSpec((SGU_GROUPS, SGU_CHUNK, 1), lambda i: (0, 0, 0)),
            pl.BlockSpec((D_SGU, d), lambda i: (0, 0)),
            pl.BlockSpec((tm, d), lambda i: (i, 0)),
            pl.BlockSpec((1, 8, d), mod_map),
        ],
        out_specs=pl.BlockSpec((tm, d), lambda i: (i, 0)),
        out_shape=jax.ShapeDtypeStruct((n_rows, d), F32),
        scratch_shapes=[pltpu.VMEM((tm, D_SGU), BF16)],
        compiler_params=_cparams(("arbitrary",)),
        name="sgu_mix",
    )(u, v, w_s, b_s, w_out, h, mod)


def _mixer_sgu(h, mod, mod_map, g1, p, n_rows):
    u, v = _sgu_in(h, mod, mod_map, g1, p["w_in"], p["b_in"], p["ln_g"], p["ln_b"], n_rows)
    return _sgu_mix(u, v, p["w_s"], p["b_s"], p["w_out"], h, mod, mod_map, n_rows)


def _pool_kernel(n_lat_tiles, h_ref, m_ref, g_ref, w_ref, b_ref, ls_ref, o_ref):
    i = pl.program_id(0)
    tm, d = h_ref.shape
    m = m_ref[0]
    h = h_ref[...]
    ms = jnp.mean(h * h, axis=-1, keepdims=True)
    inv = lax.rsqrt(ms + NORM_EPS)
    seq_len = jnp.where(i < n_lat_tiles, SEQ, CTX_LEN)
    pos = lax.broadcasted_iota(jnp.int32, (tm, 1), 0) & (seq_len - 1)
    for gi, w in enumerate(POOL_WINDOWS):
        cols = slice(gi * POOL_GD, (gi + 1) * POOL_GD)
        u = (h[:, cols] * inv) * g_ref[:, cols]
        u = u * (1.0 + m[1:2, cols]) + m[0:1, cols]
        total = jnp.zeros_like(u)
        count = jnp.zeros((tm, 1), F32)
        for k in range(-(w // 2), w // 2):
            ok = jnp.logical_and(pos + k >= 0, pos + k < seq_len)
            tap = u if k == 0 else pltpu.roll(u, (-k) % tm, axis=0)
            total = total + jnp.where(ok, tap, 0.0)
            count = count + ok.astype(F32)
        pooled = total / count - u
        y = _dot(pooled.astype(BF16), w_ref[gi]) + b_ref[gi]
        y = y * ls_ref[:, cols]
        o_ref[:, cols] = h[:, cols] + m[2:3, cols] * y


def _mixer_pool(h, mod, g1, p, layer, n_mod_rows, n_batch, n_rows):
    tm = SEQ
    d = h.shape[1]
    return pl.pallas_call(
        functools.partial(_pool_kernel, n_batch),
        grid=(n_rows // tm,),
        in_specs=[
            pl.BlockSpec((tm, d), lambda i: (i, 0)),
            pl.BlockSpec((1, 8, d), _mod_index(layer, n_mod_rows, n_batch, tm)),
            pl.BlockSpec((1, d), lambda i: (0, 0)),
            pl.BlockSpec((len(POOL_WINDOWS), POOL_GD, POOL_GD), lambda i: (0, 0, 0)),
            pl.BlockSpec((len(POOL_WINDOWS), 1, POOL_GD), lambda i: (0, 0, 0)),
            pl.BlockSpec((1, d), lambda i: (0, 0)),
        ],
        out_specs=pl.BlockSpec((tm, d), lambda i: (i, 0)),
        out_shape=jax.ShapeDtypeStruct((n_rows, d), F32),
        compiler_params=_cparams(("arbitrary",)),
        name="pool_mix",
    )(h, mod, g1, p["w"], p["b"], p["ls"])


def _qkv_kernel(h_ref, m_ref, g_ref, w_ref, gain_ref, hsum_ref, hexp_ref, q_ref, k_ref, v_ref, x_s):
    j = pl.program_id(1)

    @pl.when(j == 0)
    def _():
        m = m_ref[0]
        x_s[...] = _normmod(h_ref[...], g_ref[...], m[0:1], m[1:2]).astype(BF16)

    z = _dot(x_s[...], w_ref[...])

    def head_norm(gain):
        ss = _dot((z * z).astype(BF16), hsum_ref[...])
        ss_hi, ss_lo = _split_bf16(ss)
        ms = (_dot(ss_hi, hexp_ref[...]) + _dot(ss_lo, hexp_ref[...])) * (1.0 / NA_HD)
        return (z * lax.rsqrt(ms + NORM_EPS)) * gain

    @pl.when(j == 0)
    def _():
        q_ref[...] = head_norm(gain_ref[0:1]).astype(BF16)

    @pl.when(j == 1)
    def _():
        k_ref[...] = head_norm(gain_ref[1:2]).astype(BF16)

    @pl.when(j == 2)
    def _():
        v_ref[...] = z.astype(BF16)


def _qkv(h, mod, mod_map, g1, w_qkv, gains, hsum, hexp, n_rows):
    tm = TM_TOK
    d = h.shape[1]
    out = jax.ShapeDtypeStruct((n_rows, d), BF16)
    return pl.pallas_call(
        _qkv_kernel,
        grid=(n_rows // tm, 3),
        in_specs=[
            pl.BlockSpec((tm, d), lambda i, j: (i, 0)),
            pl.BlockSpec((1, 8, d), mod_map),
            pl.BlockSpec((1, d), lambda i, j: (0, 0)),
            pl.BlockSpec((d, d), lambda i, j: (0, j)),
            pl.BlockSpec((2, d), lambda i, j: (0, 0)),
            pl.BlockSpec((d, LANES), lambda i, j: (0, 0)),
            pl.BlockSpec((LANES, d), lambda i, j: (0, 0)),
        ],
        out_specs=[pl.BlockSpec((tm, d), lambda i, j: (i, 0))] * 3,
        out_shape=[out, out, out],
        scratch_shapes=[pltpu.VMEM((tm, d), BF16)],
        compiler_params=_cparams(("arbitrary", "arbitrary")),
        name="na_qkv",
    )(h, mod, g1, w_qkv, gains, hsum, hexp)


def _na_kernel(q_ref, k_ref, v_ref, kc_ref, vc_ref, bias_ref, o_ref):
    lane = lax.broadcasted_iota(jnp.int32, (GRID_W, 2 * NA_HD), 1)
    kc = kc_ref[...]
    vc = vc_ref[...]

    def row_body(r, carry):
        r0 = jnp.clip(r - NA_KH // 2, 0, NA_ROWS - NA_KH)
        off = r - r0
        q = q_ref[pl.ds(pl.multiple_of(r * GRID_W, GRID_W), GRID_W), :]
        key_rows = pl.ds(pl.multiple_of(r0 * GRID_W, GRID_W), NA_KH * GRID_W)
        kw = k_ref[key_rows, :]
        vw = v_ref[key_rows, :]
        outs = []
        for hh in range(2):
            qm = jnp.where((lane >= NA_HD) == (hh == 1), q, jnp.zeros_like(q))
            s_nb = _dot_nt(qm, kw) + bias_ref[hh, off]
            s_cx = _dot_nt(qm, kc)
            mx = jnp.maximum(jnp.max(s_nb, axis=-1, keepdims=True), jnp.max(s_cx, axis=-1, keepdims=True))
            p_nb = jnp.exp(s_nb - mx)
            p_cx = jnp.exp(s_cx - mx)
            denom = jnp.sum(p_nb, axis=-1, keepdims=True) + jnp.sum(p_cx, axis=-1, keepdims=True)
            o = _dot(p_nb.astype(BF16), vw) + _dot(p_cx.astype(BF16), vc)
            outs.append(o / denom)
        o = jnp.where(lane >= NA_HD, outs[1], outs[0])
        o_ref[pl.ds(pl.multiple_of(r * GRID_W, GRID_W), GRID_W), :] = o.astype(BF16)
        return carry

    lax.fori_loop(0, NA_ROWS, row_body, 0)


def _na_attn(q, k, v, bias, n_batch):
    hp = 2 * NA_HD
    ctx_blk0 = n_batch * SEQ // CTX_LEN
    return pl.pallas_call(
        _na_kernel,
        grid=(n_batch, NA_HEADS // 2),
        in_specs=[
            pl.BlockSpec((SEQ, hp), lambda b, p: (b, p)),
            pl.BlockSpec((SEQ, hp), lambda b, p: (b, p)),
            pl.BlockSpec((SEQ, hp), lambda b, p: (b, p)),
            pl.BlockSpec((CTX_LEN, hp), lambda b, p: (ctx_blk0 + b, p)),
            pl.BlockSpec((CTX_LEN, hp), lambda b, p: (ctx_blk0 + b, p)),
            pl.BlockSpec((2, NA_KH, GRID_W, NA_KH * GRID_W), lambda b, p: (p, 0, 0, 0)),
        ],
        out_specs=pl.BlockSpec((SEQ, hp), lambda b, p: (b, p)),
        out_shape=jax.ShapeDtypeStruct((n_batch * SEQ, D_MODEL), BF16),
        compiler_params=_cparams(("arbitrary", "arbitrary")),
        name="na_attn",
    )(q, k, v, k, v, bias)


def _na_bias_table(rpb):
    col = np.arange(GRID_W)
    col_start = np.clip(col - NA_KW // 2, 0, GRID_W - NA_KW)
    in_win = (col[None, :] >= col_start[:, None]) & (col[None, :] < col_start[:, None] + NA_KW)
    dc_idx = np.clip(col[None, :] - col[:, None], -(NA_KW - 1), NA_KW - 1) + NA_KW - 1
    off = np.arange(NA_KH)
    a = np.arange(NA_KH)
    dr_idx = a[None, :] - off[:, None] + NA_KH - 1
    t = rpb.astype(F32)[:, dr_idx]
    t = t[:, :, :, dc_idx]
    t = jnp.where(jnp.asarray(in_win)[None, None, None], t, NEG_INF)
    t = jnp.transpose(t, (0, 1, 3, 2, 4))
    return t.reshape(NA_HEADS, NA_KH, GRID_W, NA_KH * GRID_W)


def _mixer_na(h, mod, mod_map, g1, p, n_batch, n_rows):
    n_lat = n_batch * SEQ
    q, k, v = _qkv(h, mod, mod_map, g1, p["w_qkv"], p["gains"], p["hsum"], p["hexp"], n_rows)
    o = _na_attn(q, k, v, p["bias"], n_batch)
    return _mm_res(o, p["w_o"], h, mod, mod_map, n_lat, 2, "na_out")


def kernel(x, c, ctx, c_ctx, mod_w, mod_b, norm1_g, norm2_g, lru_w_in, lru_conv_w, lru_conv_b, lru_w_a, lru_b_a, lru_w_i, lru_b_i, lru_lambda, lru_w_out, sgu_w_in, sgu_b_in, sgu_ln_g, sgu_ln_b, sgu_w_s, sgu_b_s, sgu_w_out, pool_w, pool_b, pool_scale, na_w_qkv, na_q_g, na_k_g, na_rpb, na_w_o, ffn_w1, ffn_w3, ffn_w2, moe_w_r, moe_b_r, moe_w1, moe_w3, moe_w2):
    n_batch, seq, d = x.shape
    assert (seq, d, ctx.shape[1]) == (SEQ, D_MODEL, CTX_LEN)
    n_lat = n_batch * SEQ
    n_ctx = n_batch * CTX_LEN
    n_rows = n_lat + n_ctx
    assert n_ctx % SEQ == 0, "context rows must fill whole row tiles"
    tm_ffn = TM_FFN

    h = jnp.concatenate([x.reshape(n_lat, d), ctx.reshape(n_ctx, d)], axis=0)

    n_mod_rows = -(-(n_batch + 1) // SUBLANES) * SUBLANES
    c_all = jnp.concatenate([c, c_ctx[None, :], jnp.zeros((n_mod_rows - n_batch - 1, d), F32)], axis=0)
    mod = _mod_table(c_all, mod_w, mod_b)

    def mod_map(layer, tm):
        return _mod_index(layer, n_mod_rows, n_batch, tm)

    bf = lambda t: t.astype(BF16)
    row = lambda t: t.reshape(1, -1).astype(F32)

    wp, gate_bias = _lru_gate_weights(lru_w_a[0], lru_b_a[0], lru_w_i[0], lru_b_i[0])
    lru = dict(w_in=bf(lru_w_in[0]), conv_w=lru_conv_w[0], conv_b=row(lru_conv_b[0]), wp=wp, bias=gate_bias,
               lam=lru_lambda[0], w_out=bf(lru_w_out[0]))
    h = _mixer_rglru(h, mod, mod_map(0, TM_TOK), row(norm1_g[0]), lru, n_batch, n_rows)
    h = _ffn(h, mod, mod_map(0, tm_ffn), row(norm2_g[0]), bf(ffn_w1[0]), bf(ffn_w3[0]), bf(ffn_w2[0]), n_rows, tm_ffn)

    sgu = dict(w_in=bf(sgu_w_in[0]), b_in=row(sgu_b_in[0]), ln_g=row(sgu_ln_g[0]), ln_b=row(sgu_ln_b[0]),
               w_s=bf(sgu_w_s[0]), b_s=sgu_b_s[0].reshape(SGU_GROUPS, SGU_CHUNK, 1), w_out=bf(sgu_w_out[0]))
    h = _mixer_sgu(h, mod, mod_map(1, TM_TOK), row(norm1_g[1]), sgu, n_rows)
    h = _moe(h, mod, mod_map(1, TM_TOK), row(norm2_g[1]), moe_w_r[0], moe_b_r[0],
             bf(moe_w1[0]), bf(moe_w3[0]), bf(moe_w2[0]), n_rows, tm_ffn)

    pool = dict(w=bf(pool_w[0]), b=pool_b[0].reshape(len(POOL_WINDOWS), 1, POOL_GD), ls=row(pool_scale[0]))
    h = _mixer_pool(h, mod, row(norm1_g[2]), pool, 2, n_mod_rows, n_batch, n_rows)
    h = _ffn(h, mod, mod_map(2, tm_ffn), row(norm2_g[2]), bf(ffn_w1[1]), bf(ffn_w3[1]), bf(ffn_w2[1]), n_rows, tm_ffn)

    head_of_col = np.arange(D_MODEL) // NA_HD
    hsum = jnp.asarray(head_of_col[:, None] == np.arange(LANES)[None, :], BF16)
    hexp = jnp.asarray(np.arange(LANES)[:, None] == head_of_col[None, :], BF16)
    gains = jnp.stack([jnp.tile(na_q_g[0], NA_HEADS) * (NA_HD ** -0.5), jnp.tile(na_k_g[0], NA_HEADS)]).astype(F32)
    na = dict(w_qkv=bf(na_w_qkv[0]), gains=gains, hsum=hsum, hexp=hexp, bias=_na_bias_table(na_rpb[0]),
              w_o=bf(na_w_o[0]))
    h = _mixer_na(h, mod, mod_map(3, TM_TOK), row(norm1_g[3]), na, n_batch, n_rows)
    h = _moe(h, mod, mod_map(3, TM_TOK), row(norm2_g[3]), moe_w_r[1], moe_b_r[1],
             bf(moe_w1[1]), bf(moe_w3[1]), bf(moe_w2[1]), n_lat, tm_ffn)
    return h.reshape(n_batch, SEQ, d)
```

```python
import functools

import jax
import jax.numpy as jnp
import numpy as np
from jax import lax
from jax.experimental import pallas as pl
from jax.experimental.pallas import tpu as pltpu

F32 = jnp.float32
BF16 = jnp.bfloat16

D_MODEL = 1024
SEQ = 2048
CTX_LEN = 256
GRID_W = 64
NORM_EPS = 1e-6
NEG_INF = -1e30

D_RNN = 1280
LRU_BLOCKS = 8
LRU_BW = D_RNN // LRU_BLOCKS
LRU_HALF = D_RNN // 2
CONV_W = 4
LRU_C = 8.0
SCAN_CHUNK = 256

D_SGU = 2 * D_MODEL
SGU_CHUNK = 128
SGU_GROUPS = 8
SGU_GD = D_SGU // SGU_GROUPS

POOL_WINDOWS = (2, 4, 8, 16)
POOL_GD = D_MODEL // len(POOL_WINDOWS)

NA_HEADS = 16
NA_HD = D_MODEL // NA_HEADS
NA_KH = 8
NA_KW = 16
NA_ROWS = SEQ // GRID_W

D_FF = 7 * D_MODEL // 2
N_EXPERTS = 8
TOP_K = 2

LANES = 128
SUBLANES = 8
MXU_DIM = 256
VMEM_LIMIT = 56 * 2**20

TM_TOK = 512
TM_FFN = 1024
TF_FFN = 512
TF_MOE = 896
NA_GROUP = 4
NA_UNION = NA_GROUP + NA_KH - 1


def _cparams(sem):
    return pltpu.CompilerParams(dimension_semantics=sem, vmem_limit_bytes=VMEM_LIMIT)


def _dot(a, b):
    return jnp.dot(a, b, preferred_element_type=F32)


def _dot_nt(a, b):
    return lax.dot_general(a, b, (((1,), (1,)), ((), ())), preferred_element_type=F32)


def _split_bf16(x):
    hi = x.astype(BF16)
    lo = (x - hi.astype(F32)).astype(BF16)
    return hi, lo


def _normmod(h, g, shift, scale):
    ms = jnp.mean(h * h, axis=-1, keepdims=True)
    y = (h * lax.rsqrt(ms + NORM_EPS)) * g
    return y * (1.0 + scale) + shift


def _mod_index(layer, n_mod_rows, n_batch, tm):
    def index_map(i, *_):
        return (layer * n_mod_rows + jnp.minimum((i * tm) // SEQ, n_batch), 0, 0)
    return index_map


def _mod_kernel(s_ref, w_ref, b_ref, o_ref):
    s = s_ref[...]
    s = s * jax.nn.sigmoid(s)
    s_hi, s_lo = _split_bf16(s)
    w_hi, w_lo = _split_bf16(w_ref[0])
    acc = _dot(s_hi, w_hi) + _dot(s_lo, w_hi) + _dot(s_hi, w_lo)
    o_ref[0] = acc + b_ref[0]


def _mod_table(c_all, mod_w, mod_b):
    depth, d, n6 = mod_w.shape
    rows = c_all.shape[0]
    tn = 1024
    out = pl.pallas_call(
        _mod_kernel,
        grid=(depth, n6 // tn),
        in_specs=[
            pl.BlockSpec((rows, d), lambda l, j: (0, 0)),
            pl.BlockSpec((1, d, tn), lambda l, j: (l, 0, j)),
            pl.BlockSpec((1, 1, tn), lambda l, j: (l, 0, j)),
        ],
        out_specs=pl.BlockSpec((1, rows, tn), lambda l, j: (l, 0, j)),
        out_shape=jax.ShapeDtypeStruct((depth, rows, n6), F32),
        compiler_params=_cparams(("arbitrary", "arbitrary")),
        name="mod_table",
    )(c_all, mod_w, mod_b.reshape(depth, 1, n6))
    out = out.reshape(depth, rows, 6, d)
    out = jnp.pad(out, ((0, 0), (0, 0), (0, 2), (0, 0)))
    return out.reshape(depth * rows, 8, d)


def _mm_res_kernel(gate_row, y_ref, w_ref, h_ref, m_ref, o_ref):
    m = m_ref[0]
    o_ref[...] = h_ref[...] + m[gate_row:gate_row + 1] * _dot(y_ref[...], w_ref[...])


def _mm_res(y, w, h, mod, mod_map, n_rows, gate_row, name):
    tm = TM_TOK
    k, n = w.shape
    return pl.pallas_call(
        functools.partial(_mm_res_kernel, gate_row),
        grid=(n_rows // tm,),
        in_specs=[
            pl.BlockSpec((tm, k), lambda i: (i, 0)),
            pl.BlockSpec((k, n), lambda i: (0, 0)),
            pl.BlockSpec((tm, n), lambda i: (i, 0)),
            pl.BlockSpec((1, 8, n), mod_map),
        ],
        out_specs=pl.BlockSpec((tm, n), lambda i: (i, 0)),
        out_shape=jax.ShapeDtypeStruct((n_rows, n), F32),
        compiler_params=_cparams(("arbitrary",)),
        name=name,
    )(y, w, h, mod)


def _ffn_kernel(nf, h_ref, m_ref, g_ref, w1_ref, w3_ref, w2_ref, o_ref, f_s, acc_s):
    j = pl.program_id(1)

    @pl.when(j == 0)
    def _():
        m = m_ref[0]
        f_s[...] = _normmod(h_ref[...], g_ref[...], m[3:4], m[4:5]).astype(BF16)
        acc_s[...] = jnp.zeros_like(acc_s)

    f = f_s[...]
    h1 = _dot(f, w1_ref[...])
    h3 = _dot(f, w3_ref[...])
    hid = (h1 * jax.nn.sigmoid(h1) * h3).astype(BF16)
    acc_s[...] += _dot(hid, w2_ref[...])

    @pl.when(j == nf - 1)
    def _():
        o_ref[...] = h_ref[...] + m_ref[0][5:6] * acc_s[...]


def _ffn(h, mod, mod_map, g2, w1, w3, w2, n_rows, tm):
    d, dff = w1.shape
    tf = TF_FFN
    nf = dff // tf
    return pl.pallas_call(
        functools.partial(_ffn_kernel, nf),
        grid=(n_rows // tm, nf),
        in_specs=[
            pl.BlockSpec((tm, d), lambda i, j: (i, 0)),
            pl.BlockSpec((1, 8, d), mod_map),
            pl.BlockSpec((1, d), lambda i, j: (0, 0)),
            pl.BlockSpec((d, tf), lambda i, j: (0, j)),
            pl.BlockSpec((d, tf), lambda i, j: (0, j)),
            pl.BlockSpec((tf, d), lambda i, j: (j, 0)),
        ],
        out_specs=pl.BlockSpec((tm, d), lambda i, j: (i, 0)),
        out_shape=jax.ShapeDtypeStruct((n_rows, d), F32),
        scratch_shapes=[pltpu.VMEM((tm, d), BF16), pltpu.VMEM((tm, d), F32)],
        compiler_params=_cparams(("arbitrary", "arbitrary")),
        name="ffn_dense",
    )(h, mod, g2, w1, w3, w2)


def _router_kernel(h_ref, m_ref, g_ref, wh_ref, wl_ref, b_ref, f_ref, ti_ref, tw_ref):
    m = m_ref[0]
    f = _normmod(h_ref[...], g_ref[...], m[3:4], m[4:5])
    f_ref[...] = f
    f_hi, f_lo = _split_bf16(f)
    logits = _dot(f_hi, wh_ref[...]) + _dot(f_lo, wh_ref[...]) + _dot(f_hi, wl_ref[...])
    logits = logits + b_ref[...]
    lane = lax.broadcasted_iota(jnp.int32, logits.shape, 1)
    v1 = jnp.max(logits, axis=-1, keepdims=True)
    i1 = jnp.min(jnp.where(logits == v1, lane, LANES), axis=-1, keepdims=True)
    rest = jnp.where(lane == i1, -jnp.inf, logits)
    v2 = jnp.max(rest, axis=-1, keepdims=True)
    i2 = jnp.min(jnp.where(rest == v2, lane, LANES), axis=-1, keepdims=True)
    e2 = jnp.exp(v2 - v1)
    w1 = 1.0 / (1.0 + e2)
    w2 = e2 / (1.0 + e2)
    col = lax.broadcasted_iota(jnp.int32, ti_ref.shape, 1)
    ti_ref[...] = jnp.where(col == 0, i1, i2)
    tw_ref[...] = jnp.where(col == 0, w1, w2)


def _router(h, mod, mod_map, g2, wr_hi, wr_lo, br, n_rows):
    tm = TM_TOK
    d = h.shape[1]
    return pl.pallas_call(
        _router_kernel,
        grid=(n_rows // tm,),
        in_specs=[
            pl.BlockSpec((tm, d), lambda i: (i, 0)),
            pl.BlockSpec((1, 8, d), mod_map),
            pl.BlockSpec((1, d), lambda i: (0, 0)),
            pl.BlockSpec((d, LANES), lambda i: (0, 0)),
            pl.BlockSpec((d, LANES), lambda i: (0, 0)),
            pl.BlockSpec((1, LANES), lambda i: (0, 0)),
        ],
        out_specs=[
            pl.BlockSpec((tm, d), lambda i: (i, 0)),
            pl.BlockSpec((tm, TOP_K), lambda i: (i, 0)),
            pl.BlockSpec((tm, TOP_K), lambda i: (i, 0)),
        ],
        out_shape=[
            jax.ShapeDtypeStruct((n_rows, d), F32),
            jax.ShapeDtypeStruct((n_rows, TOP_K), jnp.int32),
            jax.ShapeDtypeStruct((n_rows, TOP_K), F32),
        ],
        compiler_params=_cparams(("arbitrary",)),
        name="moe_router",
    )(h, mod, g2, wr_hi, wr_lo, br)


def _moe_kernel(nf, tm, n_vis, te_ref, src0_ref, srcn_ref, dst_ref, gw_ref, f_hbm, w1_ref, w3_ref, w2_ref,
                o_hbm, xg_s, x_s, acc_s, y_s, sem_g, sem_s):
    i = pl.program_id(0)
    j = pl.program_id(1)
    rows_per_step = tm // nf
    slot = i % 2
    other = 1 - slot

    def gather_row(src_ref, r, buf):
        return pltpu.make_async_copy(f_hbm.at[pl.ds(src_ref[0, 0, r], 1)], xg_s.at[buf, pl.ds(r, 1)], sem_g.at[buf])

    def scatter_row(r, buf):
        return pltpu.make_async_copy(y_s.at[buf, pl.ds(r, 1)], o_hbm.at[pl.ds(dst_ref[0, 0, r], 1)], sem_s.at[buf])

    def wait_all_rows(buf_ref, sem):
        pltpu.make_async_copy(buf_ref, buf_ref, sem).wait()

    @pl.when(jnp.logical_and(i == 0, j == 0))
    def _():
        def start(r, c):
            gather_row(src0_ref, r, 0).start()
            return c
        lax.fori_loop(0, tm, start, 0, unroll=8)
        y_s[1] = jnp.zeros((tm, y_s.shape[2]), F32)

    @pl.when(j == 0)
    def _():
        wait_all_rows(xg_s.at[slot], sem_g.at[slot])
        x_s[...] = xg_s[slot].astype(BF16)
        acc_s[...] = jnp.zeros_like(acc_s)

    x = x_s[...]
    h1 = _dot(x, w1_ref[0])
    h3 = _dot(x, w3_ref[0])
    hid = (h1 * jax.nn.sigmoid(h1) * h3).astype(BF16)
    acc_s[...] += _dot(hid, w2_ref[0])
    base = j * rows_per_step
    for k in range(rows_per_step):
        gather_row(srcn_ref, base + k, other).start()
        scatter_row(base + k, other).start()

    @pl.when(j == nf - 1)
    def _():
        @pl.when(i >= 1)
        def _():
            wait_all_rows(y_s.at[slot], sem_s.at[slot])
        y_s[slot] = gw_ref[...] * acc_s[...]

        @pl.when(i == n_vis)
        def _():
            wait_all_rows(xg_s.at[other], sem_g.at[other])
            wait_all_rows(y_s.at[other], sem_s.at[other])


def _moe_plan(top_i, top_w, tm):
    n_tok = top_i.shape[0]
    n_asg = n_tok * TOP_K
    n_tc = n_asg // tm
    n_vis = n_tc + N_EXPERTS
    experts = jnp.arange(N_EXPERTS, dtype=jnp.int32)
    flat_e = top_i.reshape(-1).astype(jnp.int32)
    keys = flat_e * n_asg + jnp.arange(n_asg, dtype=jnp.int32)
    s_keys, s_gate = lax.sort((keys, top_w.reshape(-1)), num_keys=1)
    s_asg = s_keys % n_asg
    counts = jnp.sum((flat_e[:, None] == experts[None, :]).astype(jnp.int32), axis=0)
    end = jnp.cumsum(counts)
    start = end - counts
    first_tile = start // tm
    n_e_vis = jnp.where(counts > 0, (end - 1) // tm - first_tile + 1, 0)
    vis_end = jnp.cumsum(n_e_vis)
    vis_off = vis_end - n_e_vis
    total_vis = vis_end[-1]
    v = jnp.arange(n_vis, dtype=jnp.int32)
    active = v < total_vis
    e_raw = jnp.sum((vis_end[None, :] <= v[:, None]).astype(jnp.int32), axis=1)
    e_last = jnp.sum((vis_end <= total_vis - 1).astype(jnp.int32))
    e_v = jnp.minimum(jnp.where(active, e_raw, e_last), N_EXPERTS - 1)
    pick = lambda table: jnp.sum(jnp.where(e_v[:, None] == experts[None, :], table[None, :], 0), axis=1)
    t_v = jnp.where(active, pick(first_tile) + v - pick(vis_off), 0)
    q = t_v[:, None] * tm + jnp.arange(tm, dtype=jnp.int32)[None, :]
    in_group = (q >= pick(start)[:, None]) & (q < pick(end)[:, None]) & active[:, None]
    asg = jnp.take(s_asg.reshape(n_tc, tm), t_v, axis=0)
    gate = jnp.where(in_group, jnp.take(s_gate.reshape(n_tc, tm), t_v, axis=0), 0.0)
    spare = n_asg + jnp.cumsum(jnp.logical_not(in_group).reshape(-1).astype(jnp.int32)).reshape(n_vis, tm) - 1
    src_tok = asg // TOP_K
    out_row = (asg % TOP_K) * n_tok + asg // TOP_K
    dst_row = jnp.where(in_group, out_row, spare)
    n_out = (n_vis + 1) * tm
    first_dst = n_vis * tm + jnp.arange(tm, dtype=jnp.int32)[None, :]
    dst_row = jnp.concatenate([first_dst, dst_row], axis=0)
    e_ext = jnp.concatenate([e_v, e_v[-1:]])
    return (e_ext.astype(jnp.int32), src_tok.reshape(n_vis, 1, tm).astype(jnp.int32),
            dst_row.reshape(n_vis + 1, 1, tm).astype(jnp.int32), gate.reshape(n_vis * tm, 1).astype(F32),
            n_vis, n_out)


def _moe_experts(f, plan, w1, w3, w2, tm):
    e_ext, src_tok, dst_row, gate, n_vis, n_out = plan
    _, d, dff = w1.shape
    tf = TF_MOE
    nf = dff // tf
    last = n_vis - 1

    grid_spec = pltpu.PrefetchScalarGridSpec(
        num_scalar_prefetch=1,
        grid=(n_vis + 1, nf),
        in_specs=[
            pl.BlockSpec((1, 1, tm), lambda i, j, te: (0, 0, 0), memory_space=pltpu.SMEM),
            pl.BlockSpec((1, 1, tm), lambda i, j, te: (jnp.minimum(i + 1, last), 0, 0), memory_space=pltpu.SMEM),
            pl.BlockSpec((1, 1, tm), lambda i, j, te: (i, 0, 0), memory_space=pltpu.SMEM),
            pl.BlockSpec((tm, 1), lambda i, j, te: (jnp.minimum(i, last), 0)),
            pl.BlockSpec(memory_space=pl.ANY),
            pl.BlockSpec((1, d, tf), lambda i, j, te: (te[i], 0, j)),
            pl.BlockSpec((1, d, tf), lambda i, j, te: (te[i], 0, j)),
            pl.BlockSpec((1, tf, d), lambda i, j, te: (te[i], j, 0)),
        ],
        out_specs=pl.BlockSpec(memory_space=pl.ANY),
        scratch_shapes=[
            pltpu.VMEM((2, tm, d), F32),
            pltpu.VMEM((tm, d), BF16),
            pltpu.VMEM((tm, d), F32),
            pltpu.VMEM((2, tm, d), F32),
            pltpu.SemaphoreType.DMA((2,)),
            pltpu.SemaphoreType.DMA((2,)),
        ],
    )
    return pl.pallas_call(
        functools.partial(_moe_kernel, nf, tm, n_vis),
        grid_spec=grid_spec,
        out_shape=jax.ShapeDtypeStruct((n_out, d), F32),
        compiler_params=_cparams(("arbitrary", "arbitrary")),
        name="moe_experts",
    )(e_ext, src_tok, src_tok, dst_row, gate, f, w1, w3, w2)


def _moe_combine_kernel(h_ref, m_ref, y0_ref, y1_ref, o_ref):
    o_ref[...] = h_ref[...] + m_ref[0][5:6] * (y0_ref[...] + y1_ref[...])


def _moe_combine(h, mod, mod_map, y_rows, n_rows):
    tm = TM_TOK
    d = h.shape[1]
    slot1 = n_rows // tm
    return pl.pallas_call(
        _moe_combine_kernel,
        grid=(n_rows // tm,),
        in_specs=[
            pl.BlockSpec((tm, d), lambda i: (i, 0)),
            pl.BlockSpec((1, 8, d), mod_map),
            pl.BlockSpec((tm, d), lambda i: (i, 0)),
            pl.BlockSpec((tm, d), lambda i: (slot1 + i, 0)),
        ],
        out_specs=pl.BlockSpec((tm, d), lambda i: (i, 0)),
        out_shape=jax.ShapeDtypeStruct((n_rows, d), F32),
        compiler_params=_cparams(("arbitrary",)),
        name="moe_combine",
    )(h, mod, y_rows, y_rows)


def _moe(h, mod, mod_map, g2, w_r, b_r, w1, w3, w2, n_rows, tm):
    wr_pad = jnp.pad(w_r, ((0, 0), (0, LANES - N_EXPERTS)))
    wr_hi, wr_lo = _split_bf16(wr_pad)
    br_pad = jnp.concatenate([b_r.astype(F32), jnp.full((LANES - N_EXPERTS,), NEG_INF, F32)]).reshape(1, LANES)
    f, top_i, top_w = _router(h, mod, mod_map, g2, wr_hi, wr_lo, br_pad, n_rows)
    plan = _moe_plan(top_i, top_w, tm)
    y_rows = _moe_experts(f, plan, w1, w3, w2, tm)
    return _moe_combine(h, mod, mod_map, y_rows, n_rows)


def _lru_in_kernel(h_ref, m_ref, g_ref, w_ref, xr_ref, gg_ref):
    m = m_ref[0]
    u = _normmod(h_ref[...], g_ref[...], m[0:1], m[1:2]).astype(BF16)
    xr_ref[...] = _dot(u, w_ref[:, :D_RNN])
    gg_ref[...] = jax.nn.gelu(_dot(u, w_ref[:, D_RNN:])).astype(BF16)


def _lru_in(h, mod, mod_map, g1, w_in, n_rows):
    tm = TM_TOK
    d = h.shape[1]
    return pl.pallas_call(
        _lru_in_kernel,
        grid=(n_rows // tm,),
        in_specs=[
            pl.BlockSpec((tm, d), lambda i: (i, 0)),
            pl.BlockSpec((1, 8, d), mod_map),
            pl.BlockSpec((1, d), lambda i: (0, 0)),
            pl.BlockSpec((d, 2 * D_RNN), lambda i: (0, 0)),
        ],
        out_specs=[
            pl.BlockSpec((tm, D_RNN), lambda i: (i, 0)),
            pl.BlockSpec((tm, D_RNN), lambda i: (i, 0)),
        ],
        out_shape=[
            jax.ShapeDtypeStruct((n_rows, D_RNN), F32),
            jax.ShapeDtypeStruct((n_rows, D_RNN), BF16),
        ],
        compiler_params=_cparams(("arbitrary",)),
        name="lru_in",
    )(h, mod, g1, w_in)


def _block_scan(a, b, h0, reverse):
    n, c = a.shape
    groups = n // SUBLANES
    a3 = a.reshape(groups, SUBLANES, c)
    b3 = b.reshape(groups, SUBLANES, c)
    row = lax.broadcasted_iota(jnp.int32, a3.shape, 1)
    for s in (1, 2, 4):
        if reverse:
            keep = row < SUBLANES - s
            shift = SUBLANES - s
        else:
            keep = row >= s
            shift = s
        a_prev = jnp.where(keep, pltpu.roll(a3, shift, axis=1), 1.0)
        b_prev = jnp.where(keep, pltpu.roll(b3, shift, axis=1), 0.0)
        b3 = a3 * b_prev + b3
        a3 = a3 * a_prev
    outs = [None] * groups
    h = h0
    for g in (range(groups - 1, -1, -1) if reverse else range(groups)):
        hs = b3[g] + a3[g] * h
        outs[g] = hs
        h = hs[0:1] if reverse else hs[SUBLANES - 1:SUBLANES]
    return jnp.concatenate(outs, axis=0), h


def _lru_scan_kernel(n, xr_ref, gg_ref, cw_ref, cb_ref, wp_ref, bias_ref, lam_ref, seed_ref,
                     y_ref, fin_ref, xp_s, ab_s, bb_s, hf_s):
    pad = SUBLANES
    c = xr_ref.shape[1]
    n_chunks = n // SCAN_CHUNK
    win_rows = SCAN_CHUNK + 2 * pad

    xp_s[0:pad, :] = jnp.zeros((pad, c), F32)
    xp_s[pad:pad + n, :] = xr_ref[...]
    xp_s[pad + n:2 * pad + n, :] = jnp.zeros((pad, c), F32)

    lam = lam_ref[...]
    nlam = -lam
    softplus = jnp.maximum(nlam, 0.0) + jnp.log1p(jnp.exp(-jnp.abs(nlam)))
    decay = -LRU_C * softplus
    cw = cw_ref[...]
    cb = cb_ref[...]
    bias = bias_ref[...]

    def coeffs(ci):
        win = xp_s[pl.ds(pl.multiple_of(ci * SCAN_CHUNK, SCAN_CHUNK), win_rows), :]
        xc = cb
        for k in range(CONV_W):
            shift = (CONV_W // 2 - k) % win_rows
            tap = win if shift == 0 else pltpu.roll(win, shift, axis=0)
            xc = xc + tap[pad:pad + SCAN_CHUNK] * cw[k:k + 1]
        xcb = xc.astype(BF16)
        prods = [_dot(xcb[:, LANES * w:LANES * w + MXU_DIM], wp_ref[0, w]) for w in range(4)]

        def assemble(q):
            p = [r[:, MXU_DIM * q:MXU_DIM * (q + 1)] for r in prods]
            return jnp.concatenate(
                [p[0][:, :LANES], p[0][:, LANES:] + p[1][:, :LANES], p[1][:, LANES:] + p[2][:, :LANES],
                 p[2][:, LANES:] + p[3][:, :LANES], p[3][:, LANES:]], axis=1)

        out = []
        for d in range(2):
            r = 0.5 * jnp.tanh(0.5 * (assemble(2 * d) + bias[2 * d:2 * d + 1])) + 0.5
            gi = 0.5 * jnp.tanh(0.5 * (assemble(2 * d + 1) + bias[2 * d + 1:2 * d + 2])) + 0.5
            log_a = decay[d:d + 1] * r
            a = jnp.exp(log_a)
            mult = jnp.sqrt(1.0 - a * a)
            out.append((a, mult * (gi * xc)))
        return out

    def fwd_body(ci, h):
        (a0, b0), (a1, b1) = coeffs(ci)
        rows = pl.ds(pl.multiple_of(ci * SCAN_CHUNK, SCAN_CHUNK), SCAN_CHUNK)
        ab_s[rows, :] = a1
        bb_s[rows, :] = b1
        hs, h = _block_scan(a0, b0, h, reverse=False)
        hf_s[rows, :] = hs
        return h

    h_fwd = lax.fori_loop(0, n_chunks, fwd_body, seed_ref[0, 0:1, :])

    def bwd_body(k, h):
        ci = n_chunks - 1 - k
        rows = pl.ds(pl.multiple_of(ci * SCAN_CHUNK, SCAN_CHUNK), SCAN_CHUNK)
        hs, h = _block_scan(ab_s[rows, :], bb_s[rows, :], h, reverse=True)
        y_ref[rows, :] = ((hf_s[rows, :] + hs) * gg_ref[rows, :].astype(F32)).astype(BF16)
        return h

    h_bwd = lax.fori_loop(0, n_chunks, bwd_body, seed_ref[0, 1:2, :])
    fin_ref[0] = jnp.concatenate([h_fwd, h_bwd, jnp.zeros((SUBLANES - 2, c), F32)], axis=0)


def _lru_scan(xr, gg, conv_w, conv_b, wp, bias, lam, seeds, n_batch, n, blk0, name):
    c = LRU_HALF
    seq_in = lambda b, hf: (blk0 + b, hf)
    return pl.pallas_call(
        functools.partial(_lru_scan_kernel, n),
        grid=(n_batch, 2),
        in_specs=[
            pl.BlockSpec((n, c), seq_in),
            pl.BlockSpec((n, c), seq_in),
            pl.BlockSpec((CONV_W, c), lambda b, hf: (0, hf)),
            pl.BlockSpec((1, c), lambda b, hf: (0, hf)),
            pl.BlockSpec((1, 4, MXU_DIM, 4 * MXU_DIM), lambda b, hf: (hf, 0, 0, 0)),
            pl.BlockSpec((4, c), lambda b, hf: (0, hf)),
            pl.BlockSpec((2, c), lambda b, hf: (0, hf)),
            pl.BlockSpec((1, SUBLANES, c), lambda b, hf: (b, 0, hf)),
        ],
        out_specs=[
            pl.BlockSpec((n, c), lambda b, hf: (b, hf)),
            pl.BlockSpec((1, SUBLANES, c), lambda b, hf: (b, 0, hf)),
        ],
        out_shape=[
            jax.ShapeDtypeStruct((n_batch * n, D_RNN), BF16),
            jax.ShapeDtypeStruct((n_batch, SUBLANES, D_RNN), F32),
        ],
        scratch_shapes=[
            pltpu.VMEM((n + 2 * SUBLANES, c), F32),
            pltpu.VMEM((n, c), F32),
            pltpu.VMEM((n, c), F32),
            pltpu.VMEM((n, c), F32),
        ],
        compiler_params=_cparams(("arbitrary", "arbitrary")),
        name=name,
    )(xr, gg, conv_w, conv_b, wp, bias, lam, seeds)


def _lru_gate_weights(w_a, b_a, w_i, b_i):
    kinds = [w_a[0], w_i[0], w_a[1], w_i[1]]
    windows = []
    for w in range(4):
        off = (LRU_BW - LANES) * w
        room = MXU_DIM - LRU_BW - off
        tiles = [jnp.pad(kw.reshape(2, 4, LRU_BW, LRU_BW)[:, w], ((0, 0), (off, room), (off, room))) for kw in kinds]
        windows.append(jnp.concatenate(tiles, axis=-1))
    wp = jnp.stack(windows, axis=1)
    bias = jnp.stack([b_a[0], b_i[0], b_a[1], b_i[1]], axis=0).astype(F32)
    return wp.astype(BF16), bias


def _mixer_rglru(h, mod, mod_map, g1, p, n_batch, n_rows):
    xr, gg = _lru_in(h, mod, mod_map, g1, p["w_in"], n_rows)
    zeros = jnp.zeros((n_batch, SUBLANES, D_RNN), F32)
    scan_args = (p["conv_w"], p["conv_b"], p["wp"], p["bias"], p["lam"])
    y_ctx, fin = _lru_scan(xr, gg, *scan_args, zeros, n_batch, CTX_LEN, n_batch * SEQ // CTX_LEN, "lru_scan_ctx")
    y_lat, _ = _lru_scan(xr, gg, *scan_args, fin, n_batch, SEQ, 0, "lru_scan_lat")
    y = jnp.concatenate([y_lat, y_ctx], axis=0)
    return _mm_res(y, p["w_out"], h, mod, mod_map, n_rows, 2, "lru_out")


def _sgu_in_kernel(h_ref, m_ref, g_ref, w_ref, b_ref, lng_ref, lnb_ref, u_ref, v_ref, x_s):
    j = pl.program_id(1)

    @pl.when(j == 0)
    def _():
        m = m_ref[0]
        x_s[...] = _normmod(h_ref[...], g_ref[...], m[0:1], m[1:2]).astype(BF16)

    z = jax.nn.gelu(_dot(x_s[...], w_ref[...]) + b_ref[...])

    @pl.when(j == 0)
    def _():
        u_ref[...] = z.astype(BF16)

    @pl.when(j == 1)
    def _():
        mu = jnp.mean(z, axis=-1, keepdims=True)
        zc = z - mu
        var = jnp.mean(zc * zc, axis=-1, keepdims=True)
        v_ref[...] = (zc * lax.rsqrt(var + NORM_EPS) * lng_ref[...] + lnb_ref[...]).astype(BF16)


def _sgu_in(h, mod, mod_map, g1, w_in, b_in, ln_g, ln_b, n_rows):
    tm = TM_TOK
    d = h.shape[1]
    return pl.pallas_call(
        _sgu_in_kernel,
        grid=(n_rows // tm, 2),
        in_specs=[
            pl.BlockSpec((tm, d), lambda i, j: (i, 0)),
            pl.BlockSpec((1, 8, d), mod_map),
            pl.BlockSpec((1, d), lambda i, j: (0, 0)),
            pl.BlockSpec((d, D_SGU), lambda i, j: (0, j)),
            pl.BlockSpec((1, D_SGU), lambda i, j: (0, j)),
            pl.BlockSpec((1, D_SGU), lambda i, j: (0, 0)),
            pl.BlockSpec((1, D_SGU), lambda i, j: (0, 0)),
        ],
        out_specs=[
            pl.BlockSpec((tm, D_SGU), lambda i, j: (i, 0)),
            pl.BlockSpec((tm, D_SGU), lambda i, j: (i, 0)),
        ],
        out_shape=[
            jax.ShapeDtypeStruct((n_rows, D_SGU), BF16),
            jax.ShapeDtypeStruct((n_rows, D_SGU), BF16),
        ],
        scratch_shapes=[pltpu.VMEM((tm, d), BF16)],
        compiler_params=_cparams(("arbitrary", "arbitrary")),
        name="sgu_in",
    )(h, mod, g1, w_in, b_in, ln_g, ln_b)


def _sgu_mix_kernel(u_ref, v_ref, ws_ref, bs_ref, wo_ref, h_ref, m_ref, o_ref, t_s):
    tm = u_ref.shape[0]
    for ck in range(tm // SGU_CHUNK):
        rows = slice(ck * SGU_CHUNK, (ck + 1) * SGU_CHUNK)
        for g in range(SGU_GROUPS):
            cols = slice(g * SGU_GD, (g + 1) * SGU_GD)
            mixed = _dot(ws_ref[g], v_ref[rows, cols]) + bs_ref[g]
            t_s[rows, cols] = (u_ref[rows, cols].astype(F32) * mixed).astype(BF16)
    o_ref[...] = h_ref[...] + m_ref[0][2:3] * _dot(t_s[...], wo_ref[...])


def _sgu_mix(u, v, w_s, b_s, w_out, h, mod, mod_map, n_rows):
    tm = TM_TOK
    d = h.shape[1]
    return pl.pallas_call(
        _sgu_mix_kernel,
        grid=(n_rows // tm,),
        in_specs=[
            pl.BlockSpec((tm, D_SGU), lambda i: (i, 0)),
            pl.BlockSpec((tm, D_SGU), lambda i: (i, 0)),
            pl.BlockSpec((SGU_GROUPS, SGU_CHUNK, SGU_CHUNK), lambda i: (0, 0, 0)),
            pl.BlockSpec((SGU_GROUPS, SGU_CHUNK, 1), lambda i: (0, 0, 0)),
            pl.BlockSpec((D_SGU, d), lambda i: (0, 0)),
            pl.BlockSpec((tm, d), lambda i: (i, 0)),
            pl.BlockSpec((1, 8, d), mod_map),
        ],
        out_specs=pl.BlockSpec((tm, d), lambda i: (i, 0)),
        out_shape=jax.ShapeDtypeStruct((n_rows, d), F32),
        scratch_shapes=[pltpu.VMEM((tm, D_SGU), BF16)],
        compiler_params=_cparams(("arbitrary",)),
        name="sgu_mix",
    )(u, v, w_s, b_s, w_out, h, mod)


def _mixer_sgu(h, mod, mod_map, g1, p, n_rows):
    u, v = _sgu_in(h, mod, mod_map, g1, p["w_in"], p["b_in"], p["ln_g"], p["ln_b"], n_rows)
    return _sgu_mix(u, v, p["w_s"], p["b_s"], p["w_out"], h, mod, mod_map, n_rows)


def _pool_kernel(n_lat_tiles, h_ref, m_ref, g_ref, w_ref, b_ref, ls_ref, o_ref):
    i = pl.program_id(0)
    tm, d = h_ref.shape
    m = m_ref[0]
    h = h_ref[...]
    ms = jnp.mean(h * h, axis=-1, keepdims=True)
    inv = lax.rsqrt(ms + NORM_EPS)
    seq_len = jnp.where(i < n_lat_tiles, SEQ, CTX_LEN)
    pos = lax.broadcasted_iota(jnp.int32, (tm, 1), 0) & (seq_len - 1)
    for gi, w in enumerate(POOL_WINDOWS):
        cols = slice(gi * POOL_GD, (gi + 1) * POOL_GD)
        u = (h[:, cols] * inv) * g_ref[:, cols]
        u = u * (1.0 + m[1:2, cols]) + m[0:1, cols]
        total = jnp.zeros_like(u)
        count = jnp.zeros((tm, 1), F32)
        for k in range(-(w // 2), w // 2):
            ok = jnp.logical_and(pos + k >= 0, pos + k < seq_len)
            tap = u if k == 0 else pltpu.roll(u, (-k) % tm, axis=0)
            total = total + jnp.where(ok, tap, 0.0)
            count = count + ok.astype(F32)
        pooled = total / count - u
        y = _dot(pooled.astype(BF16), w_ref[gi]) + b_ref[gi]
        y = y * ls_ref[:, cols]
        o_ref[:, cols] = h[:, cols] + m[2:3, cols] * y


def _mixer_pool(h, mod, g1, p, layer, n_mod_rows, n_batch, n_rows):
    tm = SEQ
    d = h.shape[1]
    return pl.pallas_call(
        functools.partial(_pool_kernel, n_batch),
        grid=(n_rows // tm,),
        in_specs=[
            pl.BlockSpec((tm, d), lambda i: (i, 0)),
            pl.BlockSpec((1, 8, d), _mod_index(layer, n_mod_rows, n_batch, tm)),
            pl.BlockSpec((1, d), lambda i: (0, 0)),
            pl.BlockSpec((len(POOL_WINDOWS), POOL_GD, POOL_GD), lambda i: (0, 0, 0)),
            pl.BlockSpec((len(POOL_WINDOWS), 1, POOL_GD), lambda i: (0, 0, 0)),
            pl.BlockSpec((1, d), lambda i: (0, 0)),
        ],
        out_specs=pl.BlockSpec((tm, d), lambda i: (i, 0)),
        out_shape=jax.ShapeDtypeStruct((n_rows, d), F32),
        compiler_params=_cparams(("arbitrary",)),
        name="pool_mix",
    )(h, mod, g1, p["w"], p["b"], p["ls"])


def _qkv_kernel(h_ref, m_ref, g_ref, w_ref, gain_ref, hsum_ref, hexp_ref, q_ref, k_ref, v_ref, x_s):
    j = pl.program_id(1)

    @pl.when(j == 0)
    def _():
        m = m_ref[0]
        x_s[...] = _normmod(h_ref[...], g_ref[...], m[0:1], m[1:2]).astype(BF16)

    z = _dot(x_s[...], w_ref[...])

    def head_norm(gain):
        ss = _dot((z * z).astype(BF16), hsum_ref[...])
        ss_hi, ss_lo = _split_bf16(ss)
        ms = (_dot(ss_hi, hexp_ref[...]) + _dot(ss_lo, hexp_ref[...])) * (1.0 / NA_HD)
        return (z * lax.rsqrt(ms + NORM_EPS)) * gain

    @pl.when(j == 0)
    def _():
        q_ref[...] = head_norm(gain_ref[0:1]).astype(BF16)

    @pl.when(j == 1)
    def _():
        k_ref[...] = head_norm(gain_ref[1:2]).astype(BF16)

    @pl.when(j == 2)
    def _():
        v_ref[...] = z.astype(BF16)


def _qkv(h, mod, mod_map, g1, w_qkv, gains, hsum, hexp, n_rows):
    tm = TM_TOK
    d = h.shape[1]
    out = jax.ShapeDtypeStruct((n_rows, d), BF16)
    return pl.pallas_call(
        _qkv_kernel,
        grid=(n_rows // tm, 3),
        in_specs=[
            pl.BlockSpec((tm, d), lambda i, j: (i, 0)),
            pl.BlockSpec((1, 8, d), mod_map),
            pl.BlockSpec((1, d), lambda i, j: (0, 0)),
            pl.BlockSpec((d, d), lambda i, j: (0, j)),
            pl.BlockSpec((2, d), lambda i, j: (0, 0)),
            pl.BlockSpec((d, LANES), lambda i, j: (0, 0)),
            pl.BlockSpec((LANES, d), lambda i, j: (0, 0)),
        ],
        out_specs=[pl.BlockSpec((tm, d), lambda i, j: (i, 0))] * 3,
        out_shape=[out, out, out],
        scratch_shapes=[pltpu.VMEM((tm, d), BF16)],
        compiler_params=_cparams(("arbitrary", "arbitrary")),
        name="na_qkv",
    )(h, mod, g1, w_qkv, gains, hsum, hexp)


def _na_kernel(q_ref, k_ref, v_ref, kc_ref, vc_ref, bias_ref, o_ref):
    gq = NA_GROUP * GRID_W
    lane = lax.broadcasted_iota(jnp.int32, (gq, 2 * NA_HD), 1)
    kc = kc_ref[...]
    vc = vc_ref[...]
    n_groups = NA_ROWS // NA_GROUP

    def group_body(g, carry):
        u0 = jnp.clip(g * NA_GROUP - NA_KH // 2, 0, NA_ROWS - NA_UNION)
        kind = jnp.where(g == 0, 0, jnp.where(g == n_groups - 1, 2, 1))
        q_rows = pl.ds(pl.multiple_of(g * gq, gq), gq)
        key_rows = pl.ds(pl.multiple_of(u0 * GRID_W, GRID_W), NA_UNION * GRID_W)
        q = q_ref[q_rows, :]
        zero = jnp.zeros_like(q)
        q2 = jnp.concatenate([jnp.where(lane < NA_HD, q, zero), jnp.where(lane >= NA_HD, q, zero)], axis=0)
        kw = k_ref[key_rows, :]
        vw = v_ref[key_rows, :]
        bias = jnp.concatenate([bias_ref[0, kind], bias_ref[1, kind]], axis=0)
        s_nb = _dot_nt(q2, kw) + bias
        s_cx = _dot_nt(q2, kc)
        mx = jnp.maximum(jnp.max(s_nb, axis=-1, keepdims=True), jnp.max(s_cx, axis=-1, keepdims=True))
        p_nb = jnp.exp(s_nb - mx)
        p_cx = jnp.exp(s_cx - mx)
        denom = jnp.sum(p_nb, axis=-1, keepdims=True) + jnp.sum(p_cx, axis=-1, keepdims=True)
        o = (_dot(p_nb.astype(BF16), vw) + _dot(p_cx.astype(BF16), vc)) / denom
        o_ref[q_rows, :] = jnp.where(lane < NA_HD, o[:gq], o[gq:]).astype(BF16)
        return carry

    lax.fori_loop(0, n_groups, group_body, 0)


def _na_attn(q, k, v, bias, n_batch):
    hp = 2 * NA_HD
    ctx_blk0 = n_batch * SEQ // CTX_LEN
    return pl.pallas_call(
        _na_kernel,
        grid=(NA_HEADS // 2, n_batch),
        in_specs=[
            pl.BlockSpec((SEQ, hp), lambda p, b: (b, p)),
            pl.BlockSpec((SEQ, hp), lambda p, b: (b, p)),
            pl.BlockSpec((SEQ, hp), lambda p, b: (b, p)),
            pl.BlockSpec((CTX_LEN, hp), lambda p, b: (ctx_blk0 + b, p)),
            pl.BlockSpec((CTX_LEN, hp), lambda p, b: (ctx_blk0 + b, p)),
            pl.BlockSpec((2, 3, NA_GROUP * GRID_W, NA_UNION * GRID_W), lambda p, b: (p, 0, 0, 0)),
        ],
        out_specs=pl.BlockSpec((SEQ, hp), lambda p, b: (b, p)),
        out_shape=jax.ShapeDtypeStruct((n_batch * SEQ, D_MODEL), BF16),
        compiler_params=_cparams(("arbitrary", "arbitrary")),
        name="na_attn",
    )(q, k, v, k, v, bias)


def _na_bias_table(rpb):
    col = np.arange(GRID_W)
    col_start = np.clip(col - NA_KW // 2, 0, GRID_W - NA_KW)
    col_ok = (col[None, :] >= col_start[:, None]) & (col[None, :] < col_start[:, None] + NA_KW)
    dc_idx = np.clip(col[None, :] - col[:, None], -(NA_KW - 1), NA_KW - 1) + NA_KW - 1
    n_groups = NA_ROWS // NA_GROUP
    dr_idx = np.zeros((3, NA_GROUP, NA_UNION), np.int64)
    row_ok = np.zeros((3, NA_GROUP, NA_UNION), bool)
    for kind, g in enumerate((0, 1, n_groups - 1)):
        u0 = int(np.clip(g * NA_GROUP - NA_KH // 2, 0, NA_ROWS - NA_UNION))
        for k in range(NA_GROUP):
            r = g * NA_GROUP + k
            r0 = int(np.clip(r - NA_KH // 2, 0, NA_ROWS - NA_KH))
            for a in range(NA_UNION):
                ok = r0 <= u0 + a < r0 + NA_KH
                row_ok[kind, k, a] = ok
                dr_idx[kind, k, a] = (u0 + a - r + NA_KH - 1) if ok else 0
    t = rpb.astype(F32)[:, dr_idx]
    t = t[..., dc_idx]
    ok = row_ok[None, :, :, :, None, None] & col_ok[None, None, None, None]
    t = jnp.where(jnp.asarray(ok), t, NEG_INF)
    t = jnp.transpose(t, (0, 1, 2, 4, 3, 5))
    return t.reshape(NA_HEADS, 3, NA_GROUP * GRID_W, NA_UNION * GRID_W)


def _mixer_na(h, mod, mod_map, g1, p, n_batch, n_rows):
    n_lat = n_batch * SEQ
    q, k, v = _qkv(h, mod, mod_map, g1, p["w_qkv"], p["gains"], p["hsum"], p["hexp"], n_rows)
    o = _na_attn(q, k, v, p["bias"], n_batch)
    return _mm_res(o, p["w_o"], h, mod, mod_map, n_lat, 2, "na_out")


def kernel(x, c, ctx, c_ctx, mod_w, mod_b, norm1_g, norm2_g, lru_w_in, lru_conv_w, lru_conv_b, lru_w_a, lru_b_a, lru_w_i, lru_b_i, lru_lambda, lru_w_out, sgu_w_in, sgu_b_in, sgu_ln_g, sgu_ln_b, sgu_w_s, sgu_b_s, sgu_w_out, pool_w, pool_b, pool_scale, na_w_qkv, na_q_g, na_k_g, na_rpb, na_w_o, ffn_w1, ffn_w3, ffn_w2, moe_w_r, moe_b_r, moe_w1, moe_w3, moe_w2):
    n_batch, seq, d = x.shape
    assert (seq, d, ctx.shape[1]) == (SEQ, D_MODEL, CTX_LEN)
    n_lat = n_batch * SEQ
    n_ctx = n_batch * CTX_LEN
    n_rows = n_lat + n_ctx
    assert n_ctx % SEQ == 0, "context rows must fill whole row tiles"
    tm_ffn = TM_FFN

    h = jnp.concatenate([x.reshape(n_lat, d), ctx.reshape(n_ctx, d)], axis=0)

    n_mod_rows = -(-(n_batch + 1) // SUBLANES) * SUBLANES
    c_all = jnp.concatenate([c, c_ctx[None, :], jnp.zeros((n_mod_rows - n_batch - 1, d), F32)], axis=0)
    mod = _mod_table(c_all, mod_w, mod_b)

    def mod_map(layer, tm):
        return _mod_index(layer, n_mod_rows, n_batch, tm)

    bf = lambda t: t.astype(BF16)
    row = lambda t: t.reshape(1, -1).astype(F32)

    wp, gate_bias = _lru_gate_weights(lru_w_a[0], lru_b_a[0], lru_w_i[0], lru_b_i[0])
    lru = dict(w_in=bf(lru_w_in[0]), conv_w=lru_conv_w[0], conv_b=row(lru_conv_b[0]), wp=wp, bias=gate_bias,
               lam=lru_lambda[0], w_out=bf(lru_w_out[0]))
    h = _mixer_rglru(h, mod, mod_map(0, TM_TOK), row(norm1_g[0]), lru, n_batch, n_rows)
    h = _ffn(h, mod, mod_map(0, tm_ffn), row(norm2_g[0]), bf(ffn_w1[0]), bf(ffn_w3[0]), bf(ffn_w2[0]), n_rows, tm_ffn)

    sgu = dict(w_in=bf(sgu_w_in[0]), b_in=row(sgu_b_in[0]), ln_g=row(sgu_ln_g[0]), ln_b=row(sgu_ln_b[0]),
               w_s=bf(sgu_w_s[0]), b_s=sgu_b_s[0].reshape(SGU_GROUPS, SGU_CHUNK, 1), w_out=bf(sgu_w_out[0]))
    h = _mixer_sgu(h, mod, mod_map(1, TM_TOK), row(norm1_g[1]), sgu, n_rows)
    h = _moe(h, mod, mod_map(1, TM_TOK), row(norm2_g[1]), moe_w_r[0], moe_b_r[0],
             bf(moe_w1[0]), bf(moe_w3[0]), bf(moe_w2[0]), n_rows, tm_ffn)

    pool = dict(w=bf(pool_w[0]), b=pool_b[0].reshape(len(POOL_WINDOWS), 1, POOL_GD), ls=row(pool_scale[0]))
    h = _mixer_pool(h, mod, row(norm1_g[2]), pool, 2, n_mod_rows, n_batch, n_rows)
    h = _ffn(h, mod, mod_map(2, tm_ffn), row(norm2_g[2]), bf(ffn_w1[1]), bf(ffn_w3[1]), bf(ffn_w2[1]), n_rows, tm_ffn)

    head_of_col = np.arange(D_MODEL) // NA_HD
    hsum = jnp.asarray(head_of_col[:, None] == np.arange(LANES)[None, :], BF16)
    hexp = jnp.asarray(np.arange(LANES)[:, None] == head_of_col[None, :], BF16)
    gains = jnp.stack([jnp.tile(na_q_g[0], NA_HEADS) * (NA_HD ** -0.5), jnp.tile(na_k_g[0], NA_HEADS)]).astype(F32)
    na = dict(w_qkv=bf(na_w_qkv[0]), gains=gains, hsum=hsum, hexp=hexp, bias=_na_bias_table(na_rpb[0]),
              w_o=bf(na_w_o[0]))
    h = _mixer_na(h, mod, mod_map(3, TM_TOK), row(norm1_g[3]), na, n_batch, n_rows)
    h = _moe(h, mod, mod_map(3, TM_TOK), row(norm2_g[3]), moe_w_r[1], moe_b_r[1],
             bf(moe_w1[1]), bf(moe_w3[1]), bf(moe_w2[1]), n_lat, tm_ffn)
    return h.reshape(n_batch, SEQ, d)
```
